```python
import math
import jax, jax.numpy as jnp
from jax import lax
import numpy as np

D_MODEL = 2048
BATCH = 4
SEQ = 2048
DEPTH = 1
DEC_BATCH = 128
DEC_SEQ = 4
PAST_LEN = 2048
PAGE_SIZE = 128

ATTN_WIDTH = D_MODEL // 2
POOL_WIDTH = D_MODEL - ATTN_WIDTH
N_HEADS = 8
HEAD_DIM = ATTN_WIDTH // (2 * N_HEADS)
V_DIM = 2 * HEAD_DIM
POOL_WINDOWS = (2, 4, 8, 16)
N_POOL_GROUPS = len(POOL_WINDOWS)
POOL_GROUP = POOL_WIDTH // N_POOL_GROUPS
POOL_BUF = max(POOL_WINDOWS) - 1
IN_WIDTH = 3 * ATTN_WIDTH + POOL_WIDTH
D_FF = ((8 * D_MODEL // 3 + 127) // 128) * 128
FFN_RESIDUAL = 0.5
Q_BLOCK = 128
NORM_EPS = 1e-6
SUBLN_EPS = 1e-5
MASK_VALUE = -1e30

kernel_name = "hybrid_diffattn_pool_macaron_step"


def rmsnorm(x, g, eps=NORM_EPS):
    xf = x.astype(jnp.float32)
    y = xf * lax.rsqrt(jnp.mean(xf * xf, axis=-1, keepdims=True) + eps)
    return (y * g.astype(jnp.float32)).astype(x.dtype)


def half_step_ffn(x, g, w_gate, w_up, w_down):
    h = rmsnorm(x, g)
    return x + FFN_RESIDUAL * ((jax.nn.silu(h @ w_gate) * (h @ w_up)) @ w_down)


def lambda_init(layer):
    return 0.8 - 0.6 * math.exp(-0.3 * layer)


def diff_lambda(lq1, lk1, lq2, lk2, lam_init):
    f = jnp.float32
    return (jnp.exp(jnp.sum(lq1.astype(f) * lk1.astype(f)))
            - jnp.exp(jnp.sum(lq2.astype(f) * lk2.astype(f))) + lam_init)


def mixer_projections(x, g, w_in):
    B, T, _ = x.shape
    p = rmsnorm(x, g) @ w_in
    q = p[..., :ATTN_WIDTH].reshape(B, T, N_HEADS, V_DIM)
    k = p[..., ATTN_WIDTH:2 * ATTN_WIDTH].reshape(B, T, N_HEADS, V_DIM)
    v = p[..., 2 * ATTN_WIDTH:3 * ATTN_WIDTH].reshape(B, T, N_HEADS, V_DIM)
    u = p[..., 3 * ATTN_WIDTH:]
    return q, k, v, u


def diff_attend(q, k, v, mask, lam):
    scale = HEAD_DIM ** -0.5
    s1 = jnp.einsum('bqhd,bkhd->bhqk', q[..., :HEAD_DIM], k[..., :HEAD_DIM]).astype(jnp.float32) * scale
    s2 = jnp.einsum('bqhd,bkhd->bhqk', q[..., HEAD_DIM:], k[..., HEAD_DIM:]).astype(jnp.float32) * scale
    a = (jax.nn.softmax(jnp.where(mask, s1, MASK_VALUE), axis=-1)
         - lam * jax.nn.softmax(jnp.where(mask, s2, MASK_VALUE), axis=-1))
    return jnp.einsum('bhqk,bkhd->bqhd', a.astype(v.dtype), v)


def prompt_attention(q, k, v, lam):
    B, T = q.shape[:2]
    n_blk = T // Q_BLOCK
    qb = jnp.moveaxis(q.reshape(B, n_blk, Q_BLOCK, N_HEADS, V_DIM), 1, 0)
    kpos = jnp.arange(T)

    def block(args):
        qi, start = args
        qpos = start + jnp.arange(Q_BLOCK)
        return diff_attend(qi, k, v, kpos[None, :] <= qpos[:, None], lam)

    o = lax.map(block, (qb, jnp.arange(n_blk) * Q_BLOCK))
    return jnp.moveaxis(o, 0, 1).reshape(B, T, N_HEADS, V_DIM)


def sample_attention(q, k_new, v_new, k_pool, v_pool, page_table, lam):
    Bd, Tn = q.shape[:2]
    past = page_table.shape[1] * k_pool.shape[1]
    k_past = k_pool[page_table].reshape(Bd, past, N_HEADS, V_DIM)
    v_past = v_pool[page_table].reshape(Bd, past, N_HEADS, V_DIM)
    k_all = jnp.concatenate([k_past, k_new], axis=1)
    v_all = jnp.concatenate([v_past, v_new], axis=1)
    qpos = past + jnp.arange(Tn)
    kpos = jnp.arange(past + Tn)
    return diff_attend(q, k_all, v_all, kpos[None, :] <= qpos[:, None], lam)


def pool_mixer(u, prefix, pos0, w_pool, pool_scale):
    B, T, C = u.shape
    P = prefix.shape[1]
    ext = jnp.concatenate([prefix, u], axis=1).astype(jnp.float32)
    cs = jnp.concatenate([jnp.zeros((B, 1, C), jnp.float32), jnp.cumsum(ext, axis=1)], axis=1)
    cur = ext[:, P:]
    pos = pos0 + jnp.arange(T)
    outs = []
    for g, w in enumerate(POOL_WINDOWS):
        sl = slice(g * POOL_GROUP, (g + 1) * POOL_GROUP)
        win_sum = cs[:, P + 1:P + 1 + T, sl] - cs[:, P + 1 - w:P + 1 - w + T, sl]
        cnt = jnp.minimum(w, pos + 1).astype(jnp.float32)
        outs.append(win_sum / cnt[None, :, None] - cur[..., sl])
    d = jnp.stack(outs, axis=2).astype(u.dtype)
    z = jnp.einsum('btgc,gcd->btgd', d, w_pool).reshape(B, T, C) * pool_scale
    new_buf = ext[:, -POOL_BUF:].astype(u.dtype)
    return z, new_buf


def merge_heads(o, lam_init, subln_gain, z, w_out):
    B, T = o.shape[:2]
    o = rmsnorm(o, subln_gain, SUBLN_EPS) * (1.0 - lam_init)
    cat = jnp.concatenate([o.reshape(B, T, ATTN_WIDTH), z], axis=-1)
    return cat @ w_out


def setup_inputs(seed: int = 0) -> dict:
    key = jax.random.key(seed)
    ks = jax.random.split(key, 32)
    f = jnp.float32
    n_pages = PAST_LEN // PAGE_SIZE
    n_used = DEC_BATCH * n_pages
    n_phys = (n_used * 5) // 4
    nrm = lambda k, shape, s: jax.random.normal(k, shape, f) * s
    page_table = jax.random.permutation(ks[5], n_phys)[:n_used].reshape(DEC_BATCH, n_pages).astype(jnp.int32)
    return {
        "x_prompt": nrm(ks[0], (BATCH, SEQ, D_MODEL), 1.0),
        "x_sample": nrm(ks[1], (DEC_BATCH, DEC_SEQ, D_MODEL), 1.0),
        "cache_k": nrm(ks[2], (DEPTH, n_phys, PAGE_SIZE, N_HEADS, V_DIM), 1.0),
        "cache_v": nrm(ks[3], (DEPTH, n_phys, PAGE_SIZE, N_HEADS, V_DIM), 1.0),
        "state_pool": nrm(ks[4], (DEPTH, DEC_BATCH, POOL_BUF, POOL_WIDTH), 1.0),
        "page_table": page_table,
        "ffn1_norm": 1.0 + nrm(ks[6], (DEPTH, D_MODEL), 0.05),
        "ffn1_w_gate": nrm(ks[7], (DEPTH, D_MODEL, D_FF), D_MODEL ** -0.5),
        "ffn1_w_up": nrm(ks[8], (DEPTH, D_MODEL, D_FF), D_MODEL ** -0.5),
        "ffn1_w_down": nrm(ks[9], (DEPTH, D_FF, D_MODEL), D_FF ** -0.5),
        "mix_norm": 1.0 + nrm(ks[10], (DEPTH, D_MODEL), 0.05),
        "w_in": nrm(ks[11], (DEPTH, D_MODEL, IN_WIDTH), D_MODEL ** -0.5),
        "lambda_q1": nrm(ks[12], (DEPTH, HEAD_DIM), 0.1),
        "lambda_k1": nrm(ks[13], (DEPTH, HEAD_DIM), 0.1),
        "lambda_q2": nrm(ks[14], (DEPTH, HEAD_DIM), 0.1),
        "lambda_k2": nrm(ks[15], (DEPTH, HEAD_DIM), 0.1),
        "subln_gain": 1.0 + nrm(ks[16], (DEPTH, V_DIM), 0.05),
        "w_pool": nrm(ks[17], (DEPTH, N_POOL_GROUPS, POOL_GROUP, POOL_GROUP), POOL_GROUP ** -0.5),
        "pool_scale": 0.5 + nrm(ks[18], (DEPTH, POOL_WIDTH), 0.1),
        "w_out": nrm(ks[19], (DEPTH, ATTN_WIDTH + POOL_WIDTH, D_MODEL), (ATTN_WIDTH + POOL_WIDTH) ** -0.5),
        "ffn2_norm": 1.0 + nrm(ks[20], (DEPTH, D_MODEL), 0.05),
        "ffn2_w_gate": nrm(ks[21], (DEPTH, D_MODEL, D_FF), D_MODEL ** -0.5),
        "ffn2_w_up": nrm(ks[22], (DEPTH, D_MODEL, D_FF), D_MODEL ** -0.5),
        "ffn2_w_down": nrm(ks[23], (DEPTH, D_FF, D_MODEL), D_FF ** -0.5),
        "final_norm": 1.0 + nrm(ks[24], (D_MODEL,), 0.05),
    }


def reference(x_prompt, x_sample, cache_k, cache_v, state_pool, page_table,
              ffn1_norm, ffn1_w_gate, ffn1_w_up, ffn1_w_down,
              mix_norm, w_in, lambda_q1, lambda_k1, lambda_q2, lambda_k2,
              subln_gain, w_pool, pool_scale, w_out,
              ffn2_norm, ffn2_w_gate, ffn2_w_up, ffn2_w_down, final_norm):
    xp, xs = x_prompt, x_sample
    past_len = page_table.shape[1] * cache_k.shape[2]
    k_p, v_p, buf_p, k_s, v_s, buf_s = [], [], [], [], [], []
    for l in range(DEPTH):
        lam_init = lambda_init(l)
        xp = half_step_ffn(xp, ffn1_norm[l], ffn1_w_gate[l], ffn1_w_up[l], ffn1_w_down[l])
        xs = half_step_ffn(xs, ffn1_norm[l], ffn1_w_gate[l], ffn1_w_up[l], ffn1_w_down[l])
        lam = diff_lambda(lambda_q1[l], lambda_k1[l], lambda_q2[l], lambda_k2[l], lam_init)
        qp, kp, vp, up = mixer_projections(xp, mix_norm[l], w_in[l])
        op = prompt_attention(qp, kp, vp, lam)
        zp, bp = pool_mixer(up, jnp.zeros((up.shape[0], POOL_BUF, POOL_WIDTH), up.dtype), 0,
                            w_pool[l], pool_scale[l])
        xp = xp + merge_heads(op, lam_init, subln_gain[l], zp, w_out[l])
        qs, ks_, vs, us = mixer_projections(xs, mix_norm[l], w_in[l])
        os_ = sample_attention(qs, ks_, vs, cache_k[l], cache_v[l], page_table, lam)
        zs, bs = pool_mixer(us, state_pool[l], past_len, w_pool[l], pool_scale[l])
        xs = xs + merge_heads(os_, lam_init, subln_gain[l], zs, w_out[l])
        xp = half_step_ffn(xp, ffn2_norm[l], ffn2_w_gate[l], ffn2_w_up[l], ffn2_w_down[l])
        xs = half_step_ffn(xs, ffn2_norm[l], ffn2_w_gate[l], ffn2_w_up[l], ffn2_w_down[l])
        k_p.append(kp); v_p.append(vp); buf_p.append(bp)
        k_s.append(ks_); v_s.append(vs); buf_s.append(bs)
    y_prompt = rmsnorm(xp, final_norm)
    y_sample = rmsnorm(xs, final_norm)
    return (y_prompt, y_sample,
            jnp.stack(k_p), jnp.stack(v_p), jnp.stack(buf_p),
            jnp.stack(k_s), jnp.stack(v_s), jnp.stack(buf_s))
```

```python
import functools
import math

import jax
import jax.numpy as jnp
from jax import lax
from jax.experimental import pallas as pl
from jax.experimental.pallas import tpu as pltpu

F32 = jnp.float32
BF16 = jnp.bfloat16

N_HEADS = 8
POOL_WINDOWS = (2, 4, 8, 16)
POOL_BUF = max(POOL_WINDOWS) - 1
FFN_RESIDUAL = 0.5
NORM_EPS = 1e-6
SUBLN_EPS = 1e-5
MASK_VALUE = -1e30

V7X_VMEM_BYTES = 64 * 1024 * 1024
V7X_MXU_DIM = 256
LANES = 128

FFN_TOKEN_TILE = 512
FFN_HIDDEN_TILE = 512
PROJ_TOKEN_TILE = 512
ATTN_Q_TILE = 256
POOL_TOKEN_TILE = 512
POOL_HALO = 16
SAMPLE_KEY_PAD = 128


def _vmem_limit(nbytes):
    return int(min(nbytes * 5 // 4 + (8 << 20), V7X_VMEM_BYTES - (4 << 20)))


def _rmsnorm(x, g, eps):
    return x * lax.rsqrt(jnp.mean(x * x, axis=-1, keepdims=True) + eps) * g


def _lambda_init(layer):
    return 0.8 - 0.6 * math.exp(-0.3 * layer)


def _diff_lambda(lq1_ref, lk1_ref, lq2_ref, lk2_ref, lam_init):
    a = jnp.sum(lq1_ref[...] * lk1_ref[...], axis=-1, keepdims=True)
    b = jnp.sum(lq2_ref[...] * lk2_ref[...], axis=-1, keepdims=True)
    return jnp.exp(a) - jnp.exp(b) + lam_init


def _ffn_body(x_ref, g_ref, wg_ref, wu_ref, wd_ref, fg_ref, o_ref, h_ref, *, apply_final_norm):
    f = pl.program_id(1)

    @pl.when(f == 0)
    def _():
        x = x_ref[...]
        h_ref[...] = _rmsnorm(x, g_ref[...], NORM_EPS).astype(BF16)
        o_ref[...] = x

    h = h_ref[...]
    gate = jnp.dot(h, wg_ref[...], preferred_element_type=F32)
    up = jnp.dot(h, wu_ref[...], preferred_element_type=F32)
    act = (FFN_RESIDUAL * (gate * jax.nn.sigmoid(gate)) * up).astype(BF16)
    o_ref[...] += jnp.dot(act, wd_ref[...], preferred_element_type=F32)

    if apply_final_norm:
        @pl.when(f == pl.num_programs(1) - 1)
        def _():
            o_ref[...] = _rmsnorm(o_ref[...], fg_ref[...], NORM_EPS)


def _ffn(x, g, wg, wu, wd, fg, *, apply_final_norm):
    m, d = x.shape
    f_pad = wg.shape[1]
    tm = min(FFN_TOKEN_TILE, m)
    tf = FFN_HIDDEN_TILE
    assert m % tm == 0 and f_pad % tf == 0
    vmem = 2 * 2 * tm * d * 4 + tm * d * 2 + 2 * 3 * d * tf * 2 + 3 * tm * tf * 4
    return pl.pallas_call(
        functools.partial(_ffn_body, apply_final_norm=apply_final_norm),
        grid=(m // tm, f_pad // tf),
        in_specs=[
            pl.BlockSpec((tm, d), lambda i, f: (i, 0)),
            pl.BlockSpec((1, d), lambda i, f: (0, 0)),
            pl.BlockSpec((d, tf), lambda i, f: (0, f)),
            pl.BlockSpec((d, tf), lambda i, f: (0, f)),
            pl.BlockSpec((tf, d), lambda i, f: (f, 0)),
            pl.BlockSpec((1, d), lambda i, f: (0, 0)),
        ],
        out_specs=pl.BlockSpec((tm, d), lambda i, f: (i, 0)),
        out_shape=jax.ShapeDtypeStruct((m, d), F32),
        scratch_shapes=[pltpu.VMEM((tm, d), BF16)],
        compiler_params=pltpu.CompilerParams(
            dimension_semantics=("parallel", "arbitrary"),
            vmem_limit_bytes=_vmem_limit(vmem)),
        name="ffn_half_step",
    )(x, g, wg, wu, wd, fg)


def _inproj_body(x_ref, g_ref, w_ref, q_ref, k_ref, v_ref, u_ref, kb_ref, vb_ref, *, q_scale):
    h = _rmsnorm(x_ref[...], g_ref[...], NORM_EPS).astype(BF16)
    a = q_ref.shape[1]
    q_ref[...] = (jnp.dot(h, w_ref[:, 0:a], preferred_element_type=F32) * q_scale).astype(BF16)
    k = jnp.dot(h, w_ref[:, a:2 * a], preferred_element_type=F32)
    k_ref[...] = k
    kb_ref[...] = k.astype(BF16)
    v = jnp.dot(h, w_ref[:, 2 * a:3 * a], preferred_element_type=F32)
    v_ref[...] = v
    vb_ref[...] = v.astype(BF16)
    u_ref[...] = jnp.dot(h, w_ref[:, 3 * a:], preferred_element_type=F32)


def _inproj(x, g, w_in, attn_width):
    m, d = x.shape
    n = w_in.shape[1]
    a = attn_width
    pw = n - 3 * a
    head_dim = a // (2 * N_HEADS)
    tm = min(PROJ_TOKEN_TILE, m)
    assert m % tm == 0
    vmem = 2 * tm * d * 4 + d * n * 2 + 2 * tm * (a * 2 + 2 * a * 6 + pw * 4) + tm * n * 4
    row = lambda i: (i, 0)
    return pl.pallas_call(
        functools.partial(_inproj_body, q_scale=head_dim ** -0.5),
        grid=(m // tm,),
        in_specs=[
            pl.BlockSpec((tm, d), row),
            pl.BlockSpec((1, d), lambda i: (0, 0)),
            pl.BlockSpec((d, n), lambda i: (0, 0), pipeline_mode=pl.Buffered(1)),
        ],
        out_specs=[
            pl.BlockSpec((tm, a), row), pl.BlockSpec((tm, a), row), pl.BlockSpec((tm, a), row),
            pl.BlockSpec((tm, pw), row), pl.BlockSpec((tm, a), row), pl.BlockSpec((tm, a), row),
        ],
        out_shape=[
            jax.ShapeDtypeStruct((m, a), BF16), jax.ShapeDtypeStruct((m, a), F32),
            jax.ShapeDtypeStruct((m, a), F32), jax.ShapeDtypeStruct((m, pw), F32),
            jax.ShapeDtypeStruct((m, a), BF16), jax.ShapeDtypeStruct((m, a), BF16),
        ],
        compiler_params=pltpu.CompilerParams(
            dimension_semantics=("parallel",), vmem_limit_bytes=_vmem_limit(vmem)),
        name="input_projection",
    )(x, g, w_in)


def _subln(o, gain, lam_init):
    return _rmsnorm(o, gain, SUBLN_EPS) * (1.0 - lam_init)


def _prompt_attn_body(lq1_ref, lk1_ref, lq2_ref, lk2_ref, sg_ref, q_ref, k_ref, v_ref, o_ref, *,
                      lam_init):
    tq, vd = q_ref.shape
    hd = vd // 2
    qi = pl.program_id(2)
    q = q_ref[...]
    lane = lax.broadcasted_iota(jnp.int32, (tq, vd), 1)
    zero = jnp.zeros_like(q)
    qz = jnp.concatenate([jnp.where(lane < hd, q, zero), jnp.where(lane >= hd, q, zero)], axis=0)

    def tile(j, carry, *, diagonal):
        m, l, acc = carry
        start = pl.multiple_of(j * tq, tq)
        k = k_ref[pl.ds(start, tq), :]
        v = v_ref[pl.ds(start, tq), :]
        s = lax.dot_general(qz, k, (((1,), (1,)), ((), ())), preferred_element_type=F32)
        if diagonal:
            row = lax.broadcasted_iota(jnp.int32, (2 * tq, tq), 0)
            row = jnp.where(row >= tq, row - tq, row)
            col = lax.broadcasted_iota(jnp.int32, (2 * tq, tq), 1)
            s = jnp.where(col <= row, s, MASK_VALUE)
        m_new = jnp.maximum(m, jnp.max(s, axis=-1, keepdims=True))
        alpha = jnp.exp(m - m_new)
        p = jnp.exp(s - m_new)
        l = alpha * l + jnp.sum(p, axis=-1, keepdims=True)
        acc = alpha * acc + jnp.dot(p.astype(BF16), v, preferred_element_type=F32)
        return m_new, l, acc

    init = (jnp.full((2 * tq, 1), MASK_VALUE, F32), jnp.zeros((2 * tq, 1), F32),
            jnp.zeros((2 * tq, vd), F32))
    carry = lax.fori_loop(0, qi, functools.partial(tile, diagonal=False), init)
    _, l, acc = tile(qi, carry, diagonal=True)
    on = acc * (1.0 / l)
    lam = _diff_lambda(lq1_ref, lk1_ref, lq2_ref, lk2_ref, lam_init)
    o = on[:tq] - lam * on[tq:]
    o_ref[...] = _subln(o, sg_ref[...], lam_init).astype(o_ref.dtype)


def _prompt_attention(lams, sg, qb, kb, vb, *, batch, seq, lam_init):
    m, a = qb.shape
    vd = a // N_HEADS
    tq = ATTN_Q_TILE
    nq = seq // tq
    lam_spec = pl.BlockSpec((1, vd // 2), lambda b, h, i: (0, 0))
    kv_spec = pl.BlockSpec((seq, vd), lambda b, h, i: (b, h))
    q_spec = pl.BlockSpec((tq, vd), lambda b, h, i: (b * nq + i, h))
    vmem = 2 * 2 * seq * vd * 2 + 4 * tq * vd * 2 + 6 * 2 * tq * tq * 4
    return pl.pallas_call(
        functools.partial(_prompt_attn_body, lam_init=lam_init),
        grid=(batch, N_HEADS, nq),
        in_specs=[lam_spec] * 4 + [pl.BlockSpec((1, vd), lambda b, h, i: (0, 0)),
                                   q_spec, kv_spec, kv_spec],
        out_specs=q_spec,
        out_shape=jax.ShapeDtypeStruct((m, a), BF16),
        compiler_params=pltpu.CompilerParams(
            dimension_semantics=("parallel", "parallel", "arbitrary"),
            vmem_limit_bytes=_vmem_limit(vmem)),
        name="prompt_diff_attention",
    )(*lams, sg, qb, kb, vb)


def _sample_attn_body(pt_ref, lq1_ref, lk1_ref, lq2_ref, lk2_ref, sg_ref, q_ref, kn_ref, vn_ref,
                      *rest, n_pages, lam_init):
    del pt_ref
    k_pages = rest[:n_pages]
    v_pages = rest[n_pages:2 * n_pages]
    o_ref, kbf_ref, vbf_ref = rest[2 * n_pages:]
    page = k_pages[0].shape[0]
    past = n_pages * page
    n_new, a = q_ref.shape
    vd = a // N_HEADS
    hd = vd // 2
    n_keys = kbf_ref.shape[0]
    pair = 2 * vd
    assert pair == V7X_MXU_DIM and 2 * n_new == 8

    for j in range(n_pages):
        kbf_ref[j * page:(j + 1) * page, :] = k_pages[j][...].astype(BF16)
        vbf_ref[j * page:(j + 1) * page, :] = v_pages[j][...].astype(BF16)
    pad = jnp.zeros((SAMPLE_KEY_PAD - n_new, a), F32)
    kbf_ref[past:, :] = jnp.concatenate([kn_ref[...], pad], axis=0).astype(BF16)
    vbf_ref[past:, :] = jnp.concatenate([vn_ref[...], pad], axis=0).astype(BF16)

    lam = _diff_lambda(lq1_ref, lk1_ref, lq2_ref, lk2_ref, lam_init)
    qf = q_ref[...].astype(F32)
    n_rows = 4 * n_new
    r = lax.broadcasted_iota(jnp.int32, (n_rows, pair), 0)
    c = lax.broadcasted_iota(jnp.int32, (n_rows, pair), 1)
    half, head, tok = r // (2 * n_new), (r // n_new) % 2, r % n_new
    q_mask = (c // hd) == (2 * head + half)
    key = lax.broadcasted_iota(jnp.int32, (n_rows, n_keys), 1)
    tok_k = lax.broadcasted_iota(jnp.int32, (n_rows, n_keys), 0) % n_new
    visible = (key < past) | (key - past <= tok_k)
    out_r = lax.broadcasted_iota(jnp.int32, (2 * n_new, pair), 0) // n_new
    out_c = lax.broadcasted_iota(jnp.int32, (2 * n_new, pair), 1) // vd

    for n in range(a // pair):
        cols = slice(n * pair, (n + 1) * pair)
        qp = qf[:, cols]
        q_bd = jnp.where(q_mask, jnp.concatenate([qp] * 4, axis=0), 0.0).astype(BF16)
        s = lax.dot_general(q_bd, kbf_ref[:, cols], (((1,), (1,)), ((), ())),
                            preferred_element_type=F32)
        s = jnp.where(visible, s, MASK_VALUE)
        p = jnp.exp(s - jnp.max(s, axis=-1, keepdims=True))
        p = p * (1.0 / jnp.sum(p, axis=-1, keepdims=True))
        attn = (p[:2 * n_new] - lam * p[2 * n_new:]).astype(BF16)
        res = jnp.dot(attn, vbf_ref[:, cols], preferred_element_type=F32)
        res = jnp.where(out_r == out_c, res, 0.0)
        o2 = res[:, :vd] + res[:, vd:]
        o2 = _subln(o2, sg_ref[...], lam_init)
        o_ref[:, n * pair:n * pair + vd] = o2[:n_new]
        o_ref[:, n * pair + vd:(n + 1) * pair] = o2[n_new:]


def _sample_attention(page_table, lams, sg, q, k_new, v_new, k_pool, v_pool, *, lam_init):
    nb, n_new, a = q.shape
    n_pages = page_table.shape[1]
    page = k_pool.shape[1]
    vd = a // N_HEADS
    n_keys = n_pages * page + SAMPLE_KEY_PAD
    const = lambda b, pt: (0, 0)
    tok_spec = pl.BlockSpec((None, n_new, a), lambda b, pt: (b, 0, 0))
    page_specs = [pl.BlockSpec((None, page, a), lambda b, pt, j=j: (pt[b, j], 0, 0))
                  for j in range(n_pages)]
    vmem = 2 * 2 * n_pages * page * a * 4 + 2 * n_keys * a * 2 + 8 * 16 * n_keys * 4
    return pl.pallas_call(
        functools.partial(_sample_attn_body, n_pages=n_pages, lam_init=lam_init),
        grid_spec=pltpu.PrefetchScalarGridSpec(
            num_scalar_prefetch=1,
            grid=(nb,),
            in_specs=[pl.BlockSpec((1, vd // 2), const)] * 4 + [pl.BlockSpec((1, vd), const)]
                     + [tok_spec] * 3 + page_specs + page_specs,
            out_specs=tok_spec,
            scratch_shapes=[pltpu.VMEM((n_keys, a), BF16), pltpu.VMEM((n_keys, a), BF16)],
        ),
        out_shape=jax.ShapeDtypeStruct((nb, n_new, a), F32),
        compiler_params=pltpu.CompilerParams(
            dimension_semantics=("arbitrary",), vmem_limit_bytes=_vmem_limit(vmem)),
        name="sample_diff_attention",
    )(page_table, *lams, sg, q, k_new, v_new, *([k_pool] * n_pages), *([v_pool] * n_pages))


def _pool_project(d, wp_ref, scale_ref, g, cols):
    return jnp.dot(d.astype(BF16), wp_ref[g], preferred_element_type=F32) * scale_ref[:, cols]


def _prompt_pool_body(u_ref, halo_ref, wp_ref, scale_ref, z_ref, ext_ref):
    tm, c = u_ref.shape
    cg = c // len(POOL_WINDOWS)
    i = pl.program_id(1)
    ext_ref[0:POOL_HALO, :] = jnp.where(i == 0, 0.0, halo_ref[...])
    ext_ref[POOL_HALO:, :] = u_ref[...]
    pos = i * tm + lax.broadcasted_iota(jnp.int32, (tm, 1), 0)
    for g, w in enumerate(POOL_WINDOWS):
        cols = slice(g * cg, (g + 1) * cg)
        cur = ext_ref[POOL_HALO:, cols]
        win = cur
        for s in range(1, w):
            win = win + ext_ref[POOL_HALO - s:POOL_HALO - s + tm, cols]
        cnt = jnp.minimum(w, pos + 1).astype(F32)
        d = win * (1.0 / cnt) - cur
        z_ref[:, cols] = _pool_project(d, wp_ref, scale_ref, g, cols).astype(z_ref.dtype)


def _prompt_pool(u, wp, scale, *, batch, seq):
    m, c = u.shape
    tm = POOL_TOKEN_TILE
    nt = seq // tm
    halo_per_tile = tm // POOL_HALO
    const3 = lambda b, i: (0, 0, 0)
    vmem = 2 * tm * c * 4 * 2 + (tm + POOL_HALO) * c * 4 + wp.size * 2 * 2 + 4 * tm * c * 4
    return pl.pallas_call(
        _prompt_pool_body,
        grid=(batch, nt),
        in_specs=[
            pl.BlockSpec((tm, c), lambda b, i: (b * nt + i, 0)),
            pl.BlockSpec((POOL_HALO, c),
                         lambda b, i: (jnp.maximum((b * nt + i) * halo_per_tile - 1, 0), 0)),
            pl.BlockSpec(wp.shape, const3),
            pl.BlockSpec((1, c), lambda b, i: (0, 0)),
        ],
        out_specs=pl.BlockSpec((tm, c), lambda b, i: (b * nt + i, 0)),
        out_shape=jax.ShapeDtypeStruct((m, c), BF16),
        scratch_shapes=[pltpu.VMEM((tm + POOL_HALO, c), F32)],
        compiler_params=pltpu.CompilerParams(
            dimension_semantics=("parallel", "arbitrary"), vmem_limit_bytes=_vmem_limit(vmem)),
        name="prompt_pool_mixer",
    )(u, u, wp, scale)


def _sample_pool_body(state_ref, u_ref, wp_ref, scale_ref, z_ref, *, pos0):
    n_state = state_ref.shape[0]
    n_new, _, c = u_ref.shape
    cg = c // len(POOL_WINDOWS)

    def ext(r, cols):
        return state_ref[r, :, cols] if r < n_state else u_ref[r - n_state, :, cols]

    for g, w in enumerate(POOL_WINDOWS):
        cols = slice(g * cg, (g + 1) * cg)
        for t in range(n_new):
            cur = ext(n_state + t, cols)
            win = cur
            for s in range(1, w):
                win = win + ext(n_state + t - s, cols)
            d = win * (1.0 / min(w, pos0 + t + 1)) - cur
            z_ref[t, :, cols] = _pool_project(d, wp_ref, scale_ref, g, cols).astype(z_ref.dtype)


def _sample_pool(state_t, u_t, wp, scale, *, pos0):
    n_new, nb, c = u_t.shape
    vmem = 2 * (state_t.size + u_t.size) * 4 + wp.size * 4 + 4 * nb * c * 4
    return pl.pallas_call(
        functools.partial(_sample_pool_body, pos0=pos0),
        out_shape=jax.ShapeDtypeStruct((n_new, nb, c), BF16),
        compiler_params=pltpu.CompilerParams(vmem_limit_bytes=_vmem_limit(vmem)),
        name="sample_pool_mixer",
    )(state_t, u_t, wp, scale)


def _outproj_body(x_ref, o_ref, z_ref, w_ref, y_ref):
    a = o_ref.shape[1]
    y = x_ref[...] + jnp.dot(o_ref[...].astype(BF16), w_ref[0:a, :], preferred_element_type=F32)
    y_ref[...] = y + jnp.dot(z_ref[...].astype(BF16), w_ref[a:, :], preferred_element_type=F32)


def _outproj(x, o, z, w_out):
    m, d = x.shape
    a, c = o.shape[1], z.shape[1]
    tm = min(PROJ_TOKEN_TILE, m)
    assert m % tm == 0
    row = lambda i: (i, 0)
    vmem = 2 * 2 * tm * d * 4 + 2 * tm * (a + c) * 4 + w_out.size * 2 + tm * d * 4
    return pl.pallas_call(
        _outproj_body,
        grid=(m // tm,),
        in_specs=[
            pl.BlockSpec((tm, d), row), pl.BlockSpec((tm, a), row), pl.BlockSpec((tm, c), row),
            pl.BlockSpec(w_out.shape, lambda i: (0, 0), pipeline_mode=pl.Buffered(1)),
        ],
        out_specs=pl.BlockSpec((tm, d), row),
        out_shape=jax.ShapeDtypeStruct((m, d), F32),
        compiler_params=pltpu.CompilerParams(
            dimension_semantics=("parallel",), vmem_limit_bytes=_vmem_limit(vmem)),
        name="output_projection",
    )(x, o, z, w_out)


def _pad_hidden(w, axis, f_pad):
    pad = [(0, 0), (0, 0)]
    pad[axis] = (0, f_pad - w.shape[axis])
    return jnp.pad(w.astype(BF16), pad)


def kernel(x_prompt, x_sample, cache_k, cache_v, state_pool, page_table, ffn1_norm, ffn1_w_gate, ffn1_w_up, ffn1_w_down, mix_norm, w_in, lambda_q1, lambda_k1, lambda_q2, lambda_k2, subln_gain, w_pool, pool_scale, w_out, ffn2_norm, ffn2_w_gate, ffn2_w_up, ffn2_w_down, final_norm):
    batch, seq, d = x_prompt.shape
    nb, n_new, _ = x_sample.shape
    depth, n_phys, page, n_heads, vd = cache_k.shape
    assert n_heads == N_HEADS
    a = n_heads * vd
    c = state_pool.shape[-1]
    past_len = page_table.shape[1] * page
    d_ff = ffn1_w_gate.shape[-1]
    f_pad = -(-d_ff // FFN_HIDDEN_TILE) * FFN_HIDDEN_TILE

    xp = x_prompt.reshape(batch * seq, d)
    xs = x_sample.reshape(nb * n_new, d)
    fg = final_norm.reshape(1, d)
    k_p, v_p, buf_p, k_s, v_s, buf_s = [], [], [], [], [], []
    for l in range(depth):
        lam_init = _lambda_init(l)
        last = l == depth - 1
        ffn1 = (ffn1_norm[l].reshape(1, d), _pad_hidden(ffn1_w_gate[l], 1, f_pad),
                _pad_hidden(ffn1_w_up[l], 1, f_pad), _pad_hidden(ffn1_w_down[l], 0, f_pad), fg)
        ffn2 = (ffn2_norm[l].reshape(1, d), _pad_hidden(ffn2_w_gate[l], 1, f_pad),
                _pad_hidden(ffn2_w_up[l], 1, f_pad), _pad_hidden(ffn2_w_down[l], 0, f_pad), fg)
        mix_g = mix_norm[l].reshape(1, d)
        w_in_b = w_in[l].astype(BF16)
        w_out_b = w_out[l].astype(BF16)
        wp_b = w_pool[l].astype(BF16)
        scale = pool_scale[l].reshape(1, c)
        lams = tuple(p[l].reshape(1, -1) for p in (lambda_q1, lambda_k1, lambda_q2, lambda_k2))
        sg = subln_gain[l].reshape(1, vd)

        xp = _ffn(xp, *ffn1, apply_final_norm=False)
        xs = _ffn(xs, *ffn1, apply_final_norm=False)

        qb, kp, vp, up, kb, vb = _inproj(xp, mix_g, w_in_b, a)
        op = _prompt_attention(lams, sg, qb, kb, vb, batch=batch, seq=seq, lam_init=lam_init)
        zp = _prompt_pool(up, wp_b, scale, batch=batch, seq=seq)
        xp = _outproj(xp, op, zp, w_out_b)

        qs, ks, vs, us, _, _ = _inproj(xs, mix_g, w_in_b, a)
        os_ = _sample_attention(
            page_table, lams, sg, qs.reshape(nb, n_new, a), ks.reshape(nb, n_new, a),
            vs.reshape(nb, n_new, a), cache_k[l].reshape(n_phys, page, a),
            cache_v[l].reshape(n_phys, page, a), lam_init=lam_init)
        us3 = us.reshape(nb, n_new, c)
        zs_t = _sample_pool(jnp.transpose(state_pool[l], (1, 0, 2)), jnp.transpose(us3, (1, 0, 2)),
                            wp_b, scale, pos0=past_len)
        zs = jnp.transpose(zs_t, (1, 0, 2)).reshape(nb * n_new, c)
        xs = _outproj(xs, os_.reshape(nb * n_new, a), zs, w_out_b)

        xp = _ffn(xp, *ffn2, apply_final_norm=last)
        xs = _ffn(xs, *ffn2, apply_final_norm=last)

        k_p.append(kp.reshape(batch, seq, n_heads, vd))
        v_p.append(vp.reshape(batch, seq, n_heads, vd))
        buf_p.append(up.reshape(batch, seq, c)[:, seq - POOL_BUF:])
        k_s.append(ks.reshape(nb, n_new, n_heads, vd))
        v_s.append(vs.reshape(nb, n_new, n_heads, vd))
        buf_s.append(jnp.concatenate([state_pool[l], us3], axis=1)[:, -POOL_BUF:])

    return (xp.reshape(batch, seq, d), xs.reshape(nb, n_new, d),
            jnp.stack(k_p), jnp.stack(v_p), jnp.stack(buf_p),
            jnp.stack(k_s), jnp.stack(v_s), jnp.stack(buf_s))
```

```python
import functools
import math

import jax
import jax.numpy as jnp
from jax import lax
from jax.experimental import pallas as pl
from jax.experimental.pallas import tpu as pltpu

F32 = jnp.float32
BF16 = jnp.bfloat16

N_HEADS = 8
POOL_WINDOWS = (2, 4, 8, 16)
POOL_BUF = max(POOL_WINDOWS) - 1
FFN_RESIDUAL = 0.5
NORM_EPS = 1e-6
SUBLN_EPS = 1e-5
MASK_VALUE = -1e30

V7X_VMEM_BYTES = 64 * 1024 * 1024
V7X_MXU_DIM = 256
LANES = 128

FFN_TOKEN_TILE = 512
FFN_HIDDEN_TILE = 1024
PROJ_TOKEN_TILE = 512
ATTN_Q_TILE = 256
POOL_TOKEN_TILE = 512
POOL_HALO = 16
SAMPLE_KEY_PAD = 128


def _vmem_limit(nbytes):
    return int(min(nbytes * 5 // 4 + (8 << 20), V7X_VMEM_BYTES - (4 << 20)))


def _rmsnorm(x, g, eps):
    return x * lax.rsqrt(jnp.mean(x * x, axis=-1, keepdims=True) + eps) * g


def _lambda_init(layer):
    return 0.8 - 0.6 * math.exp(-0.3 * layer)


def _diff_lambda(lq1_ref, lk1_ref, lq2_ref, lk2_ref, lam_init):
    a = jnp.sum(lq1_ref[...] * lk1_ref[...], axis=-1, keepdims=True)
    b = jnp.sum(lq2_ref[...] * lk2_ref[...], axis=-1, keepdims=True)
    return jnp.exp(a) - jnp.exp(b) + lam_init


def _swiglu_partial(h, wg, wu, wd):
    gate = jnp.dot(h, wg, preferred_element_type=F32)
    up = jnp.dot(h, wu, preferred_element_type=F32)
    act = (FFN_RESIDUAL * (gate * jax.nn.sigmoid(gate)) * up).astype(BF16)
    return jnp.dot(act, wd, preferred_element_type=F32)


def _ffn_body(x_ref, g_ref, wg_ref, wu_ref, wd_ref, wg_rem_ref, wu_rem_ref, wd_rem_ref, fg_ref,
              o_ref, h_ref, *, apply_final_norm):
    f = pl.program_id(1)

    @pl.when(f == 0)
    def _():
        x = x_ref[...]
        h = _rmsnorm(x, g_ref[...], NORM_EPS).astype(BF16)
        h_ref[...] = h
        o_ref[...] = x + _swiglu_partial(h, wg_rem_ref[...], wu_rem_ref[...], wd_rem_ref[...])

    o_ref[...] += _swiglu_partial(h_ref[...], wg_ref[...], wu_ref[...], wd_ref[...])

    if apply_final_norm:
        @pl.when(f == pl.num_programs(1) - 1)
        def _():
            o_ref[...] = _rmsnorm(o_ref[...], fg_ref[...], NORM_EPS)


def _ffn(x, g, wg, wu, wd, fg, *, apply_final_norm):
    m, d = x.shape
    d_ff = wg.shape[1]
    tm = min(FFN_TOKEN_TILE, m)
    tf = FFN_HIDDEN_TILE
    n_full = d_ff // tf
    rem = d_ff - n_full * tf
    assert m % tm == 0 and n_full > 0 and rem > 0 and rem % LANES == 0
    wg_rem, wu_rem, wd_rem = wg[:, n_full * tf:], wu[:, n_full * tf:], wd[n_full * tf:, :]
    const = lambda i, f: (0, 0)
    resident = functools.partial(pl.BlockSpec, index_map=const, pipeline_mode=pl.Buffered(1))
    vmem = (2 * 2 * tm * d * 4 + tm * d * 2 + 2 * 3 * d * tf * 2 + 3 * d * rem * 2
            + 3 * tm * tf * 4)
    return pl.pallas_call(
        functools.partial(_ffn_body, apply_final_norm=apply_final_norm),
        grid=(m // tm, n_full),
        in_specs=[
            pl.BlockSpec((tm, d), lambda i, f: (i, 0)),
            pl.BlockSpec((1, d), const),
            pl.BlockSpec((d, tf), lambda i, f: (0, f)),
            pl.BlockSpec((d, tf), lambda i, f: (0, f)),
            pl.BlockSpec((tf, d), lambda i, f: (f, 0)),
            resident((d, rem)), resident((d, rem)), resident((rem, d)),
            pl.BlockSpec((1, d), const),
        ],
        out_specs=pl.BlockSpec((tm, d), lambda i, f: (i, 0)),
        out_shape=jax.ShapeDtypeStruct((m, d), F32),
        scratch_shapes=[pltpu.VMEM((tm, d), BF16)],
        compiler_params=pltpu.CompilerParams(
            dimension_semantics=("parallel", "arbitrary"),
            vmem_limit_bytes=_vmem_limit(vmem)),
        name="ffn_half_step",
    )(x, g, wg, wu, wd, wg_rem, wu_rem, wd_rem, fg)


def _inproj_body(x_ref, g_ref, w_ref, q_ref, k_ref, v_ref, u_ref, kb_ref, vb_ref, *, q_scale):
    h = _rmsnorm(x_ref[...], g_ref[...], NORM_EPS).astype(BF16)
    a = q_ref.shape[1]
    q_ref[...] = (jnp.dot(h, w_ref[:, 0:a], preferred_element_type=F32) * q_scale).astype(BF16)
    k = jnp.dot(h, w_ref[:, a:2 * a], preferred_element_type=F32)
    k_ref[...] = k
    kb_ref[...] = k.astype(BF16)
    v = jnp.dot(h, w_ref[:, 2 * a:3 * a], preferred_element_type=F32)
    v_ref[...] = v
    vb_ref[...] = v.astype(BF16)
    u_ref[...] = jnp.dot(h, w_ref[:, 3 * a:], preferred_element_type=F32)


def _inproj(x, g, w_in, attn_width):
    m, d = x.shape
    n = w_in.shape[1]
    a = attn_width
    pw = n - 3 * a
    head_dim = a // (2 * N_HEADS)
    tm = min(PROJ_TOKEN_TILE, m)
    assert m % tm == 0
    vmem = 2 * tm * d * 4 + d * n * 2 + 2 * tm * (a * 2 + 2 * a * 6 + pw * 4) + tm * n * 4
    row = lambda i: (i, 0)
    return pl.pallas_call(
        functools.partial(_inproj_body, q_scale=head_dim ** -0.5),
        grid=(m // tm,),
        in_specs=[
            pl.BlockSpec((tm, d), row),
            pl.BlockSpec((1, d), lambda i: (0, 0)),
            pl.BlockSpec((d, n), lambda i: (0, 0), pipeline_mode=pl.Buffered(1)),
        ],
        out_specs=[
            pl.BlockSpec((tm, a), row), pl.BlockSpec((tm, a), row), pl.BlockSpec((tm, a), row),
            pl.BlockSpec((tm, pw), row), pl.BlockSpec((tm, a), row), pl.BlockSpec((tm, a), row),
        ],
        out_shape=[
            jax.ShapeDtypeStruct((m, a), BF16), jax.ShapeDtypeStruct((m, a), F32),
            jax.ShapeDtypeStruct((m, a), F32), jax.ShapeDtypeStruct((m, pw), F32),
            jax.ShapeDtypeStruct((m, a), BF16), jax.ShapeDtypeStruct((m, a), BF16),
        ],
        compiler_params=pltpu.CompilerParams(
            dimension_semantics=("parallel",), vmem_limit_bytes=_vmem_limit(vmem)),
        name="input_projection",
    )(x, g, w_in)


def _subln(o, gain, lam_init):
    return _rmsnorm(o, gain, SUBLN_EPS) * (1.0 - lam_init)


def _prompt_attn_body(lq1_ref, lk1_ref, lq2_ref, lk2_ref, sg_ref, q_ref, k_ref, v_ref, o_ref, *,
                      lam_init):
    seq, vd = q_ref.shape
    hd = vd // 2
    tq = ATTN_Q_TILE
    nt = (((1,), (1,)), ((), ()))
    lam = _diff_lambda(lq1_ref, lk1_ref, lq2_ref, lk2_ref, lam_init)
    lane = lax.broadcasted_iota(jnp.int32, (tq, vd), 1)
    row = lax.broadcasted_iota(jnp.int32, (2 * tq, tq), 0)
    row = jnp.where(row >= tq, row - tq, row)
    col = lax.broadcasted_iota(jnp.int32, (2 * tq, tq), 1)
    causal = col <= row

    for i in range(seq // tq):
        lo, hi = i * tq, (i + 1) * tq
        q = q_ref[lo:hi, :]
        zero = jnp.zeros_like(q)
        qz = jnp.concatenate([jnp.where(lane < hd, q, zero), jnp.where(lane >= hd, q, zero)],
                             axis=0)
        s_diag = lax.dot_general(qz, k_ref[lo:hi, :], nt, preferred_element_type=F32)
        s_diag = jnp.where(causal, s_diag, MASK_VALUE)
        m = jnp.max(s_diag, axis=-1, keepdims=True)
        if i > 0:
            s_past = lax.dot_general(qz, k_ref[0:lo, :], nt, preferred_element_type=F32)
            m = jnp.maximum(m, jnp.max(s_past, axis=-1, keepdims=True))
        p_diag = jnp.exp(s_diag - m)
        l = jnp.sum(p_diag, axis=-1, keepdims=True)
        acc = jnp.dot(p_diag.astype(BF16), v_ref[lo:hi, :], preferred_element_type=F32)
        if i > 0:
            p_past = jnp.exp(s_past - m)
            l = l + jnp.sum(p_past, axis=-1, keepdims=True)
            acc = acc + jnp.dot(p_past.astype(BF16), v_ref[0:lo, :], preferred_element_type=F32)
        on = acc * (1.0 / l)
        o = on[:tq] - lam * on[tq:]
        o_ref[lo:hi, :] = _subln(o, sg_ref[...], lam_init).astype(o_ref.dtype)


def _prompt_attention(lams, sg, qb, kb, vb, *, batch, seq, lam_init):
    m, a = qb.shape
    vd = a // N_HEADS
    tq = ATTN_Q_TILE
    assert seq % tq == 0
    lam_spec = pl.BlockSpec((1, vd // 2), lambda b, h: (0, 0))
    head_spec = pl.BlockSpec((seq, vd), lambda b, h: (b, h))
    vmem = 2 * 4 * seq * vd * 2 + 4 * 2 * tq * seq * 4
    return pl.pallas_call(
        functools.partial(_prompt_attn_body, lam_init=lam_init),
        grid=(batch, N_HEADS),
        in_specs=[lam_spec] * 4 + [pl.BlockSpec((1, vd), lambda b, h: (0, 0)),
                                   head_spec, head_spec, head_spec],
        out_specs=head_spec,
        out_shape=jax.ShapeDtypeStruct((m, a), BF16),
        compiler_params=pltpu.CompilerParams(
            dimension_semantics=("parallel", "parallel"),
            vmem_limit_bytes=_vmem_limit(vmem)),
        name="prompt_diff_attention",
    )(*lams, sg, qb, kb, vb)


def _sample_attn_body(pt_ref, lq1_ref, lk1_ref, lq2_ref, lk2_ref, sg_ref, q_ref, kn_ref, vn_ref,
                      *rest, n_pages, lam_init):
    del pt_ref
    k_pages = rest[:n_pages]
    v_pages = rest[n_pages:2 * n_pages]
    o_ref, kbf_ref, vbf_ref = rest[2 * n_pages:]
    page = k_pages[0].shape[0]
    past = n_pages * page
    n_new, a = q_ref.shape
    vd = a // N_HEADS
    hd = vd // 2
    n_keys = kbf_ref.shape[0]
    pair = 2 * vd
    assert pair == V7X_MXU_DIM and 2 * n_new == 8

    for j in range(n_pages):
        kbf_ref[j * page:(j + 1) * page, :] = k_pages[j][...].astype(BF16)
        vbf_ref[j * page:(j + 1) * page, :] = v_pages[j][...].astype(BF16)
    pad = jnp.zeros((SAMPLE_KEY_PAD - n_new, a), F32)
    kbf_ref[past:, :] = jnp.concatenate([kn_ref[...], pad], axis=0).astype(BF16)
    vbf_ref[past:, :] = jnp.concatenate([vn_ref[...], pad], axis=0).astype(BF16)

    lam = _diff_lambda(lq1_ref, lk1_ref, lq2_ref, lk2_ref, lam_init)
    qf = q_ref[...].astype(F32)
    n_rows = 4 * n_new
    r = lax.broadcasted_iota(jnp.int32, (n_rows, pair), 0)
    c = lax.broadcasted_iota(jnp.int32, (n_rows, pair), 1)
    half, head, tok = r // (2 * n_new), (r // n_new) % 2, r % n_new
    q_mask = (c // hd) == (2 * head + half)
    key = lax.broadcasted_iota(jnp.int32, (n_rows, n_keys), 1)
    tok_k = lax.broadcasted_iota(jnp.int32, (n_rows, n_keys), 0) % n_new
    visible = (key < past) | (key - past <= tok_k)
    out_r = lax.broadcasted_iota(jnp.int32, (2 * n_new, pair), 0) // n_new
    out_c = lax.broadcasted_iota(jnp.int32, (2 * n_new, pair), 1) // vd

    for n in range(a // pair):
        cols = slice(n * pair, (n + 1) * pair)
        qp = qf[:, cols]
        q_bd = jnp.where(q_mask, jnp.concatenate([qp] * 4, axis=0), 0.0).astype(BF16)
        s = lax.dot_general(q_bd, kbf_ref[:, cols], (((1,), (1,)), ((), ())),
                            preferred_element_type=F32)
        s = jnp.where(visible, s, MASK_VALUE)
        p = jnp.exp(s - jnp.max(s, axis=-1, keepdims=True))
        p = p * (1.0 / jnp.sum(p, axis=-1, keepdims=True))
        attn = (p[:2 * n_new] - lam * p[2 * n_new:]).astype(BF16)
        res = jnp.dot(attn, vbf_ref[:, cols], preferred_element_type=F32)
        res = jnp.where(out_r == out_c, res, 0.0)
        o2 = res[:, :vd] + res[:, vd:]
        o2 = _subln(o2, sg_ref[...], lam_init)
        o_ref[:, n * pair:n * pair + vd] = o2[:n_new]
        o_ref[:, n * pair + vd:(n + 1) * pair] = o2[n_new:]


def _sample_attention(page_table, lams, sg, q, k_new, v_new, k_pool, v_pool, *, first_page,
                      lam_init):
    nb, n_new, a = q.shape
    n_pages = page_table.shape[1]
    page = k_pool.shape[1]
    vd = a // N_HEADS
    n_keys = n_pages * page + SAMPLE_KEY_PAD
    const = lambda b, pt: (0, 0)
    tok_spec = pl.BlockSpec((None, n_new, a), lambda b, pt: (b, 0, 0))
    page_specs = [pl.BlockSpec((None, page, a), lambda b, pt, j=j: (first_page + pt[b, j], 0, 0))
                  for j in range(n_pages)]
    vmem = 2 * 2 * n_pages * page * a * 4 + 2 * n_keys * a * 2 + 8 * 16 * n_keys * 4
    return pl.pallas_call(
        functools.partial(_sample_attn_body, n_pages=n_pages, lam_init=lam_init),
        grid_spec=pltpu.PrefetchScalarGridSpec(
            num_scalar_prefetch=1,
            grid=(nb,),
            in_specs=[pl.BlockSpec((1, vd // 2), const)] * 4 + [pl.BlockSpec((1, vd), const)]
                     + [tok_spec] * 3 + page_specs + page_specs,
            out_specs=tok_spec,
            scratch_shapes=[pltpu.VMEM((n_keys, a), BF16), pltpu.VMEM((n_keys, a), BF16)],
        ),
        out_shape=jax.ShapeDtypeStruct((nb, n_new, a), F32),
        compiler_params=pltpu.CompilerParams(
            dimension_semantics=("arbitrary",), vmem_limit_bytes=_vmem_limit(vmem)),
        name="sample_diff_attention",
    )(page_table, *lams, sg, q, k_new, v_new, *([k_pool] * n_pages), *([v_pool] * n_pages))


def _pool_project(d, wp_ref, scale_ref, g, cols):
    return jnp.dot(d.astype(BF16), wp_ref[g], preferred_element_type=F32) * scale_ref[:, cols]


def _prompt_pool_body(u_ref, halo_ref, wp_ref, scale_ref, z_ref, ext_ref):
    tm, c = u_ref.shape
    cg = c // len(POOL_WINDOWS)
    i = pl.program_id(1)
    ext_ref[0:POOL_HALO, :] = jnp.where(i == 0, 0.0, halo_ref[...])
    ext_ref[POOL_HALO:, :] = u_ref[...]
    pos = i * tm + lax.broadcasted_iota(jnp.int32, (tm, 1), 0)
    for g, w in enumerate(POOL_WINDOWS):
        cols = slice(g * cg, (g + 1) * cg)
        cur = ext_ref[POOL_HALO:, cols]
        win = cur
        for s in range(1, w):
            win = win + ext_ref[POOL_HALO - s:POOL_HALO - s + tm, cols]
        cnt = jnp.minimum(w, pos + 1).astype(F32)
        d = win * (1.0 / cnt) - cur
        z_ref[:, cols] = _pool_project(d, wp_ref, scale_ref, g, cols).astype(z_ref.dtype)


def _prompt_pool(u, wp, scale, *, batch, seq):
    m, c = u.shape
    tm = POOL_TOKEN_TILE
    nt = seq // tm
    halo_per_tile = tm // POOL_HALO
    const3 = lambda b, i: (0, 0, 0)
    vmem = 2 * tm * c * 4 * 2 + (tm + POOL_HALO) * c * 4 + wp.size * 2 * 2 + 4 * tm * c * 4
    return pl.pallas_call(
        _prompt_pool_body,
        grid=(batch, nt),
        in_specs=[
            pl.BlockSpec((tm, c), lambda b, i: (b * nt + i, 0)),
            pl.BlockSpec((POOL_HALO, c),
                         lambda b, i: (jnp.maximum((b * nt + i) * halo_per_tile - 1, 0), 0)),
            pl.BlockSpec(wp.shape, const3),
            pl.BlockSpec((1, c), lambda b, i: (0, 0)),
        ],
        out_specs=pl.BlockSpec((tm, c), lambda b, i: (b * nt + i, 0)),
        out_shape=jax.ShapeDtypeStruct((m, c), BF16),
        scratch_shapes=[pltpu.VMEM((tm + POOL_HALO, c), F32)],
        compiler_params=pltpu.CompilerParams(
            dimension_semantics=("parallel", "arbitrary"), vmem_limit_bytes=_vmem_limit(vmem)),
        name="prompt_pool_mixer",
    )(u, u, wp, scale)


def _sample_pool_body(state_ref, u_ref, wp_ref, scale_ref, z_ref, *, pos0):
    n_state = state_ref.shape[0]
    n_new, _, c = u_ref.shape
    cg = c // len(POOL_WINDOWS)

    def ext(r, cols):
        return state_ref[r, :, cols] if r < n_state else u_ref[r - n_state, :, cols]

    for g, w in enumerate(POOL_WINDOWS):
        cols = slice(g * cg, (g + 1) * cg)
        for t in range(n_new):
            cur = ext(n_state + t, cols)
            win = cur
            for s in range(1, w):
                win = win + ext(n_state + t - s, cols)
            d = win * (1.0 / min(w, pos0 + t + 1)) - cur
            z_ref[t, :, cols] = _pool_project(d, wp_ref, scale_ref, g, cols).astype(z_ref.dtype)


def _sample_pool(state_t, u_t, wp, scale, *, pos0):
    n_new, nb, c = u_t.shape
    vmem = 2 * (state_t.size + u_t.size) * 4 + wp.size * 4 + 4 * nb * c * 4
    return pl.pallas_call(
        functools.partial(_sample_pool_body, pos0=pos0),
        out_shape=jax.ShapeDtypeStruct((n_new, nb, c), BF16),
        compiler_params=pltpu.CompilerParams(vmem_limit_bytes=_vmem_limit(vmem)),
        name="sample_pool_mixer",
    )(state_t, u_t, wp, scale)


def _outproj_body(x_ref, o_ref, z_ref, w_ref, y_ref):
    a = o_ref.shape[1]
    y = x_ref[...] + jnp.dot(o_ref[...].astype(BF16), w_ref[0:a, :], preferred_element_type=F32)
    y_ref[...] = y + jnp.dot(z_ref[...].astype(BF16), w_ref[a:, :], preferred_element_type=F32)


def _outproj(x, o, z, w_out):
    m, d = x.shape
    a, c = o.shape[1], z.shape[1]
    tm = min(PROJ_TOKEN_TILE, m)
    assert m % tm == 0
    row = lambda i: (i, 0)
    vmem = 2 * 2 * tm * d * 4 + 2 * tm * (a + c) * 4 + w_out.size * 2 + tm * d * 4
    return pl.pallas_call(
        _outproj_body,
        grid=(m // tm,),
        in_specs=[
            pl.BlockSpec((tm, d), row), pl.BlockSpec((tm, a), row), pl.BlockSpec((tm, c), row),
            pl.BlockSpec(w_out.shape, lambda i: (0, 0), pipeline_mode=pl.Buffered(1)),
        ],
        out_specs=pl.BlockSpec((tm, d), row),
        out_shape=jax.ShapeDtypeStruct((m, d), F32),
        compiler_params=pltpu.CompilerParams(
            dimension_semantics=("parallel",), vmem_limit_bytes=_vmem_limit(vmem)),
        name="output_projection",
    )(x, o, z, w_out)


def kernel(x_prompt, x_sample, cache_k, cache_v, state_pool, page_table, ffn1_norm, ffn1_w_gate, ffn1_w_up, ffn1_w_down, mix_norm, w_in, lambda_q1, lambda_k1, lambda_q2, lambda_k2, subln_gain, w_pool, pool_scale, w_out, ffn2_norm, ffn2_w_gate, ffn2_w_up, ffn2_w_down, final_norm):
    batch, seq, d = x_prompt.shape
    nb, n_new, _ = x_sample.shape
    depth, n_phys, page, n_heads, vd = cache_k.shape
    assert n_heads == N_HEADS
    a = n_heads * vd
    c = state_pool.shape[-1]
    past_len = page_table.shape[1] * page
    k_pool = cache_k.reshape(depth * n_phys, page, a)
    v_pool = cache_v.reshape(depth * n_phys, page, a)

    xp = x_prompt.reshape(batch * seq, d)
    xs = x_sample.reshape(nb * n_new, d)
    fg = final_norm.reshape(1, d)
    k_p, v_p, buf_p, k_s, v_s, buf_s = [], [], [], [], [], []
    for l in range(depth):
        lam_init = _lambda_init(l)
        last = l == depth - 1
        ffn1 = (ffn1_norm[l].reshape(1, d), ffn1_w_gate[l].astype(BF16), ffn1_w_up[l].astype(BF16),
                ffn1_w_down[l].astype(BF16), fg)
        ffn2 = (ffn2_norm[l].reshape(1, d), ffn2_w_gate[l].astype(BF16), ffn2_w_up[l].astype(BF16),
                ffn2_w_down[l].astype(BF16), fg)
        mix_g = mix_norm[l].reshape(1, d)
        w_in_b = w_in[l].astype(BF16)
        w_out_b = w_out[l].astype(BF16)
        wp_b = w_pool[l].astype(BF16)
        scale = pool_scale[l].reshape(1, c)
        lams = tuple(p[l].reshape(1, -1) for p in (lambda_q1, lambda_k1, lambda_q2, lambda_k2))
        sg = subln_gain[l].reshape(1, vd)

        xp = _ffn(xp, *ffn1, apply_final_norm=False)
        xs = _ffn(xs, *ffn1, apply_final_norm=False)

        qb, kp, vp, up, kb, vb = _inproj(xp, mix_g, w_in_b, a)
        op = _prompt_attention(lams, sg, qb, kb, vb, batch=batch, seq=seq, lam_init=lam_init)
        zp = _prompt_pool(up, wp_b, scale, batch=batch, seq=seq)
        xp = _outproj(xp, op, zp, w_out_b)

        qs, ks, vs, us, _, _ = _inproj(xs, mix_g, w_in_b, a)
        os_ = _sample_attention(
            page_table, lams, sg, qs.reshape(nb, n_new, a), ks.reshape(nb, n_new, a),
            vs.reshape(nb, n_new, a), k_pool, v_pool, first_page=l * n_phys, lam_init=lam_init)
        us3 = us.reshape(nb, n_new, c)
        zs_t = _sample_pool(jnp.transpose(state_pool[l], (1, 0, 2)), jnp.transpose(us3, (1, 0, 2)),
                            wp_b, scale, pos0=past_len)
        zs = jnp.transpose(zs_t, (1, 0, 2)).reshape(nb * n_new, c)
        xs = _outproj(xs, os_.reshape(nb * n_new, a), zs, w_out_b)

        xp = _ffn(xp, *ffn2, apply_final_norm=last)
        xs = _ffn(xs, *ffn2, apply_final_norm=last)

        k_p.append(kp.reshape(batch, seq, n_heads, vd))
        v_p.append(vp.reshape(batch, seq, n_heads, vd))
        buf_p.append(up.reshape(batch, seq, c)[:, seq - POOL_BUF:])
        k_s.append(ks.reshape(nb, n_new, n_heads, vd))
        v_s.append(vs.reshape(nb, n_new, n_heads, vd))
        buf_s.append(jnp.concatenate([state_pool[l], us3], axis=1)[:, -POOL_BUF:])

    return (xp.reshape(batch, seq, d), xs.reshape(nb, n_new, d),
            jnp.stack(k_p), jnp.stack(v_p), jnp.stack(buf_p),
            jnp.stack(k_s), jnp.stack(v_s), jnp.stack(buf_s))
```

```python
import functools
import math

import jax
import jax.numpy as jnp
from jax import lax
from jax.experimental import pallas as pl
from jax.experimental.pallas import tpu as pltpu

F32 = jnp.float32
BF16 = jnp.bfloat16

N_HEADS = 8
POOL_WINDOWS = (2, 4, 8, 16)
POOL_BUF = max(POOL_WINDOWS) - 1
FFN_RESIDUAL = 0.5
NORM_EPS = 1e-6
SUBLN_EPS = 1e-5
MASK_VALUE = -1e30

V7X_VMEM_BYTES = 64 * 1024 * 1024
V7X_MXU_DIM = 256
LANES = 128

FFN_TOKEN_TILE = 512
FFN_HIDDEN_TILE = 1024
FFN_EMIT_HIDDEN_TILE = 256
PROJ_TOKEN_TILE = 512
ATTN_Q_TILE = 256
POOL_TOKEN_TILE = 512
POOL_HALO = 16
SAMPLE_KEY_PAD = 128


def _vmem_limit(nbytes):
    return int(min(nbytes * 5 // 4 + (8 << 20), V7X_VMEM_BYTES - (4 << 20)))


def _rmsnorm(x, g, eps):
    return x * lax.rsqrt(jnp.mean(x * x, axis=-1, keepdims=True) + eps) * g


def _lambda_init(layer):
    return 0.8 - 0.6 * math.exp(-0.3 * layer)


def _diff_lambda(lq1_ref, lk1_ref, lq2_ref, lk2_ref, lam_init):
    a = jnp.sum(lq1_ref[...] * lk1_ref[...], axis=-1, keepdims=True)
    b = jnp.sum(lq2_ref[...] * lk2_ref[...], axis=-1, keepdims=True)
    return jnp.exp(a) - jnp.exp(b) + lam_init


def _swiglu_partial(h, wg, wu, wd):
    gate = jnp.dot(h, wg, preferred_element_type=F32)
    up = jnp.dot(h, wu, preferred_element_type=F32)
    act = (FFN_RESIDUAL * (gate * jax.nn.sigmoid(gate)) * up).astype(BF16)
    return jnp.dot(act, wd, preferred_element_type=F32)


def _ffn_body(x_ref, g_ref, wg_ref, wu_ref, wd_ref, wg_rem_ref, wu_rem_ref, wd_rem_ref, fg_ref,
              o_ref, *rest, apply_final_norm, emit_bf16_weights):
    f = pl.program_id(1)
    h_ref = rest[-1]

    @pl.when(f == 0)
    def _():
        x = x_ref[...]
        h = _rmsnorm(x, g_ref[...], NORM_EPS).astype(BF16)
        h_ref[...] = h
        o_ref[...] = x + _swiglu_partial(h, wg_rem_ref[...], wu_rem_ref[...], wd_rem_ref[...])

    if emit_bf16_weights:
        wg, wu, wd = (r[...].astype(BF16) for r in (wg_ref, wu_ref, wd_ref))
        for out_ref, w in zip(rest[:3], (wg, wu, wd)):
            out_ref[...] = w
    else:
        wg, wu, wd = wg_ref[...], wu_ref[...], wd_ref[...]
    o_ref[...] += _swiglu_partial(h_ref[...], wg, wu, wd)

    if apply_final_norm:
        @pl.when(f == pl.num_programs(1) - 1)
        def _():
            o_ref[...] = _rmsnorm(o_ref[...], fg_ref[...], NORM_EPS)


def _ffn(x, g, wg, wu, wd, w_rem, fg, *, layer, n_main, apply_final_norm,
         emit_bf16_weights=False):
    m, d = x.shape
    wg_rem, wu_rem, wd_rem = w_rem
    rem = wg_rem.shape[1]
    tm = min(FFN_TOKEN_TILE, m)
    tf = FFN_EMIT_HIDDEN_TILE if emit_bf16_weights else FFN_HIDDEN_TILE
    assert m % tm == 0 and n_main % tf == 0 and rem % LANES == 0
    assert wg.dtype == (F32 if emit_bf16_weights else BF16)
    assert not emit_bf16_weights or m == tm
    const = lambda i, f: (0, 0)
    resident = functools.partial(pl.BlockSpec, index_map=const, pipeline_mode=pl.Buffered(1))
    col_tile = pl.BlockSpec((None, d, tf), lambda i, f: (layer, 0, f))
    row_tile = pl.BlockSpec((None, tf, d), lambda i, f: (layer, f, 0))
    tok_tile = pl.BlockSpec((tm, d), lambda i, f: (i, 0))
    out_specs, out_shape = [tok_tile], [jax.ShapeDtypeStruct((m, d), F32)]
    if emit_bf16_weights:
        out_specs += [pl.BlockSpec((None, d, tf), lambda i, f: (0, 0, f))] * 2
        out_specs += [pl.BlockSpec((None, tf, d), lambda i, f: (0, f, 0))]
        out_shape += [jax.ShapeDtypeStruct((1, d, n_main), BF16)] * 2
        out_shape += [jax.ShapeDtypeStruct((1, n_main, d), BF16)]
    w_bytes = wg.dtype.itemsize + (2 if emit_bf16_weights else 0)
    vmem = (2 * 2 * tm * d * 4 + tm * d * 2 + 2 * 3 * d * tf * w_bytes + 3 * d * rem * 2
            + 3 * tm * tf * 4)
    outs = pl.pallas_call(
        functools.partial(_ffn_body, apply_final_norm=apply_final_norm,
                          emit_bf16_weights=emit_bf16_weights),
        grid=(m // tm, n_main // tf),
        in_specs=[
            tok_tile,
            pl.BlockSpec((1, d), const),
            col_tile, col_tile, row_tile,
            resident((d, rem)), resident((d, rem)), resident((rem, d)),
            pl.BlockSpec((1, d), const),
        ],
        out_specs=out_specs,
        out_shape=out_shape,
        scratch_shapes=[pltpu.VMEM((tm, d), BF16)],
        compiler_params=pltpu.CompilerParams(
            dimension_semantics=("parallel", "arbitrary"),
            vmem_limit_bytes=_vmem_limit(vmem)),
        name="ffn_half_step",
    )(x, g, wg, wu, wd, wg_rem, wu_rem, wd_rem, fg)
    return outs if emit_bf16_weights else outs[0]


def _ffn_both(xp, xs, g, wg, wu, wd, fg, *, layer, apply_final_norm):
    d_ff = wg.shape[-1]
    n_main = d_ff // FFN_HIDDEN_TILE * FFN_HIDDEN_TILE
    w_rem = (wg[layer, :, n_main:].astype(BF16), wu[layer, :, n_main:].astype(BF16),
             wd[layer, n_main:, :].astype(BF16))
    xs, wg_b, wu_b, wd_b = _ffn(xs, g, wg, wu, wd, w_rem, fg, layer=layer, n_main=n_main,
                                apply_final_norm=apply_final_norm, emit_bf16_weights=True)
    xp = _ffn(xp, g, wg_b, wu_b, wd_b, w_rem, fg, layer=0, n_main=n_main,
              apply_final_norm=apply_final_norm)
    return xp, xs


def _inproj_body(x_ref, g_ref, w_ref, q_ref, k_ref, v_ref, u_ref, kb_ref, vb_ref, *, q_scale):
    h = _rmsnorm(x_ref[...], g_ref[...], NORM_EPS).astype(BF16)
    a = q_ref.shape[1]
    q_ref[...] = (jnp.dot(h, w_ref[:, 0:a], preferred_element_type=F32) * q_scale).astype(BF16)
    k = jnp.dot(h, w_ref[:, a:2 * a], preferred_element_type=F32)
    k_ref[...] = k
    kb_ref[...] = k.astype(BF16)
    v = jnp.dot(h, w_ref[:, 2 * a:3 * a], preferred_element_type=F32)
    v_ref[...] = v
    vb_ref[...] = v.astype(BF16)
    u_ref[...] = jnp.dot(h, w_ref[:, 3 * a:], preferred_element_type=F32)


def _inproj(x, g, w_in, attn_width):
    m, d = x.shape
    n = w_in.shape[1]
    a = attn_width
    pw = n - 3 * a
    head_dim = a // (2 * N_HEADS)
    tm = min(PROJ_TOKEN_TILE, m)
    assert m % tm == 0
    vmem = 2 * tm * d * 4 + d * n * 2 + 2 * tm * (a * 2 + 2 * a * 6 + pw * 4) + tm * n * 4
    row = lambda i: (i, 0)
    return pl.pallas_call(
        functools.partial(_inproj_body, q_scale=head_dim ** -0.5),
        grid=(m // tm,),
        in_specs=[
            pl.BlockSpec((tm, d), row),
            pl.BlockSpec((1, d), lambda i: (0, 0)),
            pl.BlockSpec((d, n), lambda i: (0, 0), pipeline_mode=pl.Buffered(1)),
        ],
        out_specs=[
            pl.BlockSpec((tm, a), row), pl.BlockSpec((tm, a), row), pl.BlockSpec((tm, a), row),
            pl.BlockSpec((tm, pw), row), pl.BlockSpec((tm, a), row), pl.BlockSpec((tm, a), row),
        ],
        out_shape=[
            jax.ShapeDtypeStruct((m, a), BF16), jax.ShapeDtypeStruct((m, a), F32),
            jax.ShapeDtypeStruct((m, a), F32), jax.ShapeDtypeStruct((m, pw), F32),
            jax.ShapeDtypeStruct((m, a), BF16), jax.ShapeDtypeStruct((m, a), BF16),
        ],
        compiler_params=pltpu.CompilerParams(
            dimension_semantics=("parallel",), vmem_limit_bytes=_vmem_limit(vmem)),
        name="input_projection",
    )(x, g, w_in)


def _subln(o, gain, lam_init):
    return _rmsnorm(o, gain, SUBLN_EPS) * (1.0 - lam_init)


def _prompt_attn_body(lq1_ref, lk1_ref, lq2_ref, lk2_ref, sg_ref, q_ref, k_ref, v_ref, o_ref, *,
                      lam_init):
    seq, vd = q_ref.shape
    hd = vd // 2
    tq = ATTN_Q_TILE
    nt = (((1,), (1,)), ((), ()))
    lam = _diff_lambda(lq1_ref, lk1_ref, lq2_ref, lk2_ref, lam_init)
    lane = lax.broadcasted_iota(jnp.int32, (tq, vd), 1)
    row = lax.broadcasted_iota(jnp.int32, (2 * tq, tq), 0)
    row = jnp.where(row >= tq, row - tq, row)
    col = lax.broadcasted_iota(jnp.int32, (2 * tq, tq), 1)
    causal = col <= row

    for i in range(seq // tq):
        lo, hi = i * tq, (i + 1) * tq
        q = q_ref[lo:hi, :]
        zero = jnp.zeros_like(q)
        qz = jnp.concatenate([jnp.where(lane < hd, q, zero), jnp.where(lane >= hd, q, zero)],
                             axis=0)
        s_diag = lax.dot_general(qz, k_ref[lo:hi, :], nt, preferred_element_type=F32)
        s_diag = jnp.where(causal, s_diag, MASK_VALUE)
        m = jnp.max(s_diag, axis=-1, keepdims=True)
        if i > 0:
            s_past = lax.dot_general(qz, k_ref[0:lo, :], nt, preferred_element_type=F32)
            m = jnp.maximum(m, jnp.max(s_past, axis=-1, keepdims=True))
        p_diag = jnp.exp(s_diag - m)
        l = jnp.sum(p_diag, axis=-1, keepdims=True)
        acc = jnp.dot(p_diag.astype(BF16), v_ref[lo:hi, :], preferred_element_type=F32)
        if i > 0:
            p_past = jnp.exp(s_past - m)
            l = l + jnp.sum(p_past, axis=-1, keepdims=True)
            acc = acc + jnp.dot(p_past.astype(BF16), v_ref[0:lo, :], preferred_element_type=F32)
        on = acc * (1.0 / l)
        o = on[:tq] - lam * on[tq:]
        o_ref[lo:hi, :] = _subln(o, sg_ref[...], lam_init).astype(o_ref.dtype)


def _prompt_attention(lams, sg, qb, kb, vb, *, batch, seq, lam_init):
    m, a = qb.shape
    vd = a // N_HEADS
    tq = ATTN_Q_TILE
    assert seq % tq == 0
    lam_spec = pl.BlockSpec((1, vd // 2), lambda b, h: (0, 0))
    head_spec = pl.BlockSpec((seq, vd), lambda b, h: (b, h))
    vmem = 2 * 4 * seq * vd * 2 + 4 * 2 * tq * seq * 4
    return pl.pallas_call(
        functools.partial(_prompt_attn_body, lam_init=lam_init),
        grid=(batch, N_HEADS),
        in_specs=[lam_spec] * 4 + [pl.BlockSpec((1, vd), lambda b, h: (0, 0)),
                                   head_spec, head_spec, head_spec],
        out_specs=head_spec,
        out_shape=jax.ShapeDtypeStruct((m, a), BF16),
        compiler_params=pltpu.CompilerParams(
            dimension_semantics=("parallel", "parallel"),
            vmem_limit_bytes=_vmem_limit(vmem)),
        name="prompt_diff_attention",
    )(*lams, sg, qb, kb, vb)


def _sample_attn_body(pt_ref, lq1_ref, lk1_ref, lq2_ref, lk2_ref, sg_ref, q_ref, kn_ref, vn_ref,
                      *rest, n_pages, lam_init):
    del pt_ref
    k_pages = rest[:n_pages]
    v_pages = rest[n_pages:2 * n_pages]
    o_ref, kbf_ref, vbf_ref = rest[2 * n_pages:]
    page = k_pages[0].shape[0] // N_HEADS
    past = n_pages * page
    n_new, a = q_ref.shape
    vd = a // N_HEADS
    hd = vd // 2
    n_keys = kbf_ref.shape[0]
    pair = 2 * vd
    assert pair == V7X_MXU_DIM and 2 * n_new == 8

    for j in range(n_pages):
        for h in range(N_HEADS):
            rows, cols = slice(j * page, (j + 1) * page), slice(h * vd, (h + 1) * vd)
            head_rows = pl.ds(h, page, stride=N_HEADS)
            kbf_ref[rows, cols] = k_pages[j][head_rows, :].astype(BF16)
            vbf_ref[rows, cols] = v_pages[j][head_rows, :].astype(BF16)
    pad = jnp.zeros((SAMPLE_KEY_PAD - n_new, a), F32)
    kbf_ref[past:, :] = jnp.concatenate([kn_ref[...], pad], axis=0).astype(BF16)
    vbf_ref[past:, :] = jnp.concatenate([vn_ref[...], pad], axis=0).astype(BF16)

    lam = _diff_lambda(lq1_ref, lk1_ref, lq2_ref, lk2_ref, lam_init)
    qf = q_ref[...].astype(F32)
    n_rows = 4 * n_new
    r = lax.broadcasted_iota(jnp.int32, (n_rows, pair), 0)
    c = lax.broadcasted_iota(jnp.int32, (n_rows, pair), 1)
    half, head, tok = r // (2 * n_new), (r // n_new) % 2, r % n_new
    q_mask = (c // hd) == (2 * head + half)
    key = lax.broadcasted_iota(jnp.int32, (n_rows, n_keys), 1)
    tok_k = lax.broadcasted_iota(jnp.int32, (n_rows, n_keys), 0) % n_new
    visible = (key < past) | (key - past <= tok_k)
    out_r = lax.broadcasted_iota(jnp.int32, (2 * n_new, pair), 0) // n_new
    out_c = lax.broadcasted_iota(jnp.int32, (2 * n_new, pair), 1) // vd

    for n in range(a // pair):
        cols = slice(n * pair, (n + 1) * pair)
        qp = qf[:, cols]
        q_bd = jnp.where(q_mask, jnp.concatenate([qp] * 4, axis=0), 0.0).astype(BF16)
        s = lax.dot_general(q_bd, kbf_ref[:, cols], (((1,), (1,)), ((), ())),
                            preferred_element_type=F32)
        s = jnp.where(visible, s, MASK_VALUE)
        p = jnp.exp(s - jnp.max(s, axis=-1, keepdims=True))
        p = p * (1.0 / jnp.sum(p, axis=-1, keepdims=True))
        attn = (p[:2 * n_new] - lam * p[2 * n_new:]).astype(BF16)
        res = jnp.dot(attn, vbf_ref[:, cols], preferred_element_type=F32)
        res = jnp.where(out_r == out_c, res, 0.0)
        o2 = res[:, :vd] + res[:, vd:]
        o2 = _subln(o2, sg_ref[...], lam_init)
        o_ref[:, n * pair:n * pair + vd] = o2[:n_new]
        o_ref[:, n * pair + vd:(n + 1) * pair] = o2[n_new:]


def _sample_attention(page_table, lams, sg, q, k_new, v_new, k_pool, v_pool, *, page, first_page,
                      lam_init):
    nb, n_new, a = q.shape
    n_pages = page_table.shape[1]
    vd = a // N_HEADS
    page_rows = page * N_HEADS
    n_keys = n_pages * page + SAMPLE_KEY_PAD
    const = lambda b, pt: (0, 0)
    tok_spec = pl.BlockSpec((None, n_new, a), lambda b, pt: (b, 0, 0))
    page_specs = [pl.BlockSpec((page_rows, vd), lambda b, pt, j=j: (first_page + pt[b, j], 0))
                  for j in range(n_pages)]
    vmem = 2 * 2 * n_pages * page * a * 4 + 2 * n_keys * a * 2 + 8 * 16 * n_keys * 4
    return pl.pallas_call(
        functools.partial(_sample_attn_body, n_pages=n_pages, lam_init=lam_init),
        grid_spec=pltpu.PrefetchScalarGridSpec(
            num_scalar_prefetch=1,
            grid=(nb,),
            in_specs=[pl.BlockSpec((1, vd // 2), const)] * 4 + [pl.BlockSpec((1, vd), const)]
                     + [tok_spec] * 3 + page_specs + page_specs,
            out_specs=tok_spec,
            scratch_shapes=[pltpu.VMEM((n_keys, a), BF16), pltpu.VMEM((n_keys, a), BF16)],
        ),
        out_shape=jax.ShapeDtypeStruct((nb, n_new, a), F32),
        compiler_params=pltpu.CompilerParams(
            dimension_semantics=("arbitrary",), vmem_limit_bytes=_vmem_limit(vmem)),
        name="sample_diff_attention",
    )(page_table, *lams, sg, q, k_new, v_new, *([k_pool] * n_pages), *([v_pool] * n_pages))


def _pool_project(d, wp_ref, scale_ref, g, cols):
    return jnp.dot(d.astype(BF16), wp_ref[g], preferred_element_type=F32) * scale_ref[:, cols]


def _prompt_pool_body(u_ref, halo_ref, wp_ref, scale_ref, z_ref, ext_ref):
    tm, c = u_ref.shape
    cg = c // len(POOL_WINDOWS)
    i = pl.program_id(1)
    ext_ref[0:POOL_HALO, :] = jnp.where(i == 0, 0.0, halo_ref[...])
    ext_ref[POOL_HALO:, :] = u_ref[...]
    pos = i * tm + lax.broadcasted_iota(jnp.int32, (tm, 1), 0)
    for g, w in enumerate(POOL_WINDOWS):
        cols = slice(g * cg, (g + 1) * cg)
        cur = ext_ref[POOL_HALO:, cols]
        win = cur
        for s in range(1, w):
            win = win + ext_ref[POOL_HALO - s:POOL_HALO - s + tm, cols]
        cnt = jnp.minimum(w, pos + 1).astype(F32)
        d = win * (1.0 / cnt) - cur
        z_ref[:, cols] = _pool_project(d, wp_ref, scale_ref, g, cols).astype(z_ref.dtype)


def _prompt_pool(u, wp, scale, *, batch, seq):
    m, c = u.shape
    tm = POOL_TOKEN_TILE
    nt = seq // tm
    halo_per_tile = tm // POOL_HALO
    const3 = lambda b, i: (0, 0, 0)
    vmem = 2 * tm * c * 4 * 2 + (tm + POOL_HALO) * c * 4 + wp.size * 2 * 2 + 4 * tm * c * 4
    return pl.pallas_call(
        _prompt_pool_body,
        grid=(batch, nt),
        in_specs=[
            pl.BlockSpec((tm, c), lambda b, i: (b * nt + i, 0)),
            pl.BlockSpec((POOL_HALO, c),
                         lambda b, i: (jnp.maximum((b * nt + i) * halo_per_tile - 1, 0), 0)),
            pl.BlockSpec(wp.shape, const3),
            pl.BlockSpec((1, c), lambda b, i: (0, 0)),
        ],
        out_specs=pl.BlockSpec((tm, c), lambda b, i: (b * nt + i, 0)),
        out_shape=jax.ShapeDtypeStruct((m, c), BF16),
        scratch_shapes=[pltpu.VMEM((tm + POOL_HALO, c), F32)],
        compiler_params=pltpu.CompilerParams(
            dimension_semantics=("parallel", "arbitrary"), vmem_limit_bytes=_vmem_limit(vmem)),
        name="prompt_pool_mixer",
    )(u, u, wp, scale)


def _sample_pool_body(state_ref, u_ref, wp_ref, scale_ref, z_ref, *, pos0):
    n_state = state_ref.shape[0]
    n_new, _, c = u_ref.shape
    cg = c // len(POOL_WINDOWS)

    def ext(r, cols):
        return state_ref[r, :, cols] if r < n_state else u_ref[r - n_state, :, cols]

    for g, w in enumerate(POOL_WINDOWS):
        cols = slice(g * cg, (g + 1) * cg)
        for t in range(n_new):
            cur = ext(n_state + t, cols)
            win = cur
            for s in range(1, w):
                win = win + ext(n_state + t - s, cols)
            d = win * (1.0 / min(w, pos0 + t + 1)) - cur
            z_ref[t, :, cols] = _pool_project(d, wp_ref, scale_ref, g, cols).astype(z_ref.dtype)


def _sample_pool(state_t, u_t, wp, scale, *, pos0):
    n_new, nb, c = u_t.shape
    vmem = 2 * (state_t.size + u_t.size) * 4 + wp.size * 4 + 4 * nb * c * 4
    return pl.pallas_call(
        functools.partial(_sample_pool_body, pos0=pos0),
        out_shape=jax.ShapeDtypeStruct((n_new, nb, c), BF16),
        compiler_params=pltpu.CompilerParams(vmem_limit_bytes=_vmem_limit(vmem)),
        name="sample_pool_mixer",
    )(state_t, u_t, wp, scale)


def _outproj_body(x_ref, o_ref, z_ref, w_ref, y_ref):
    a = o_ref.shape[1]
    y = x_ref[...] + jnp.dot(o_ref[...].astype(BF16), w_ref[0:a, :], preferred_element_type=F32)
    y_ref[...] = y + jnp.dot(z_ref[...].astype(BF16), w_ref[a:, :], preferred_element_type=F32)


def _outproj(x, o, z, w_out):
    m, d = x.shape
    a, c = o.shape[1], z.shape[1]
    tm = min(PROJ_TOKEN_TILE, m)
    assert m % tm == 0
    row = lambda i: (i, 0)
    vmem = 2 * 2 * tm * d * 4 + 2 * tm * (a + c) * 4 + w_out.size * 2 + tm * d * 4
    return pl.pallas_call(
        _outproj_body,
        grid=(m // tm,),
        in_specs=[
            pl.BlockSpec((tm, d), row), pl.BlockSpec((tm, a), row), pl.BlockSpec((tm, c), row),
            pl.BlockSpec(w_out.shape, lambda i: (0, 0), pipeline_mode=pl.Buffered(1)),
        ],
        out_specs=pl.BlockSpec((tm, d), row),
        out_shape=jax.ShapeDtypeStruct((m, d), F32),
        compiler_params=pltpu.CompilerParams(
            dimension_semantics=("parallel",), vmem_limit_bytes=_vmem_limit(vmem)),
        name="output_projection",
    )(x, o, z, w_out)


def kernel(x_prompt, x_sample, cache_k, cache_v, state_pool, page_table, ffn1_norm, ffn1_w_gate, ffn1_w_up, ffn1_w_down, mix_norm, w_in, lambda_q1, lambda_k1, lambda_q2, lambda_k2, subln_gain, w_pool, pool_scale, w_out, ffn2_norm, ffn2_w_gate, ffn2_w_up, ffn2_w_down, final_norm):
    batch, seq, d = x_prompt.shape
    nb, n_new, _ = x_sample.shape
    depth, n_phys, page, n_heads, vd = cache_k.shape
    assert n_heads == N_HEADS
    a = n_heads * vd
    c = state_pool.shape[-1]
    past_len = page_table.shape[1] * page
    k_pool = cache_k.reshape(depth * n_phys * page * n_heads, vd)
    v_pool = cache_v.reshape(depth * n_phys * page * n_heads, vd)

    xp = x_prompt.reshape(batch * seq, d)
    xs = x_sample.reshape(nb * n_new, d)
    fg = final_norm.reshape(1, d)
    k_p, v_p, buf_p, k_s, v_s, buf_s = [], [], [], [], [], []
    for l in range(depth):
        lam_init = _lambda_init(l)
        last = l == depth - 1
        mix_g = mix_norm[l].reshape(1, d)
        w_in_b = w_in[l].astype(BF16)
        w_out_b = w_out[l].astype(BF16)
        wp_b = w_pool[l].astype(BF16)
        scale = pool_scale[l].reshape(1, c)
        lams = tuple(p[l].reshape(1, -1) for p in (lambda_q1, lambda_k1, lambda_q2, lambda_k2))
        sg = subln_gain[l].reshape(1, vd)

        xp, xs = _ffn_both(xp, xs, ffn1_norm[l].reshape(1, d), ffn1_w_gate, ffn1_w_up, ffn1_w_down,
                           fg, layer=l, apply_final_norm=False)

        qb, kp, vp, up, kb, vb = _inproj(xp, mix_g, w_in_b, a)
        op = _prompt_attention(lams, sg, qb, kb, vb, batch=batch, seq=seq, lam_init=lam_init)
        zp = _prompt_pool(up, wp_b, scale, batch=batch, seq=seq)
        xp = _outproj(xp, op, zp, w_out_b)

        qs, ks, vs, us, _, _ = _inproj(xs, mix_g, w_in_b, a)
        os_ = _sample_attention(
            page_table, lams, sg, qs.reshape(nb, n_new, a), ks.reshape(nb, n_new, a),
            vs.reshape(nb, n_new, a), k_pool, v_pool, page=page,
            first_page=l * n_phys, lam_init=lam_init)
        us3 = us.reshape(nb, n_new, c)
        zs_t = _sample_pool(jnp.transpose(state_pool[l], (1, 0, 2)), jnp.transpose(us3, (1, 0, 2)),
                            wp_b, scale, pos0=past_len)
        zs = jnp.transpose(zs_t, (1, 0, 2)).reshape(nb * n_new, c)
        xs = _outproj(xs, os_.reshape(nb * n_new, a), zs, w_out_b)

        xp, xs = _ffn_both(xp, xs, ffn2_norm[l].reshape(1, d), ffn2_w_gate, ffn2_w_up, ffn2_w_down,
                           fg, layer=l, apply_final_norm=last)

        k_p.append(kp.reshape(batch, seq, n_heads, vd))
        v_p.append(vp.reshape(batch, seq, n_heads, vd))
        buf_p.append(up.reshape(batch, seq, c)[:, seq - POOL_BUF:])
        k_s.append(ks.reshape(nb, n_new, n_heads, vd))
        v_s.append(vs.reshape(nb, n_new, n_heads, vd))
        buf_s.append(jnp.concatenate([state_pool[l], us3], axis=1)[:, -POOL_BUF:])

    return (xp.reshape(batch, seq, d), xs.reshape(nb, n_new, d),
            jnp.stack(k_p), jnp.stack(v_p), jnp.stack(buf_p),
            jnp.stack(k_s), jnp.stack(v_s), jnp.stack(buf_s))
```

```python
import functools
import math

import jax
import jax.numpy as jnp
from jax import lax
from jax.experimental import pallas as pl
from jax.experimental.pallas import tpu as pltpu

F32 = jnp.float32
BF16 = jnp.bfloat16

N_HEADS = 8
POOL_WINDOWS = (2, 4, 8, 16)
POOL_BUF = max(POOL_WINDOWS) - 1
FFN_RESIDUAL = 0.5
NORM_EPS = 1e-6
SUBLN_EPS = 1e-5
MASK_VALUE = -1e30

V7X_VMEM_BYTES = 64 * 1024 * 1024
V7X_MXU_DIM = 256
LANES = 128

FFN_TOKEN_TILE = 512
FFN_HIDDEN_TILE = 1024
FFN_EMIT_HIDDEN_TILE = 256
PROJ_TOKEN_TILE = 512
ATTN_Q_TILE = 256
POOL_TOKEN_TILE = 512
POOL_HALO = 16
SAMPLE_KEY_PAD = 128


def _vmem_limit(nbytes):
    return int(min(nbytes * 5 // 4 + (8 << 20), V7X_VMEM_BYTES - (4 << 20)))


def _rmsnorm(x, g, eps):
    return x * lax.rsqrt(jnp.mean(x * x, axis=-1, keepdims=True) + eps) * g


def _lambda_init(layer):
    return 0.8 - 0.6 * math.exp(-0.3 * layer)


def _diff_lambda(lq1_ref, lk1_ref, lq2_ref, lk2_ref, lam_init):
    a = jnp.sum(lq1_ref[...] * lk1_ref[...], axis=-1, keepdims=True)
    b = jnp.sum(lq2_ref[...] * lk2_ref[...], axis=-1, keepdims=True)
    return jnp.exp(a) - jnp.exp(b) + lam_init


def _swiglu_partial(h, wg, wu, wd):
    gate = jnp.dot(h, wg, preferred_element_type=F32)
    up = jnp.dot(h, wu, preferred_element_type=F32)
    act = (FFN_RESIDUAL * (gate * jax.nn.sigmoid(gate)) * up).astype(BF16)
    return jnp.dot(act, wd, preferred_element_type=F32)


def _ffn_body(x_ref, g_ref, wg_ref, wu_ref, wd_ref, wg_rem_ref, wu_rem_ref, wd_rem_ref, fg_ref,
              o_ref, *rest, apply_final_norm, emit_bf16_weights):
    f = pl.program_id(1)
    h_ref = rest[-1]

    @pl.when(f == 0)
    def _():
        x = x_ref[...]
        h = _rmsnorm(x, g_ref[...], NORM_EPS).astype(BF16)
        h_ref[...] = h
        o_ref[...] = x + _swiglu_partial(h, wg_rem_ref[...], wu_rem_ref[...], wd_rem_ref[...])

    if emit_bf16_weights:
        wg, wu, wd = (r[...].astype(BF16) for r in (wg_ref, wu_ref, wd_ref))
        for out_ref, w in zip(rest[:3], (wg, wu, wd)):
            out_ref[...] = w
    else:
        wg, wu, wd = wg_ref[...], wu_ref[...], wd_ref[...]
    o_ref[...] += _swiglu_partial(h_ref[...], wg, wu, wd)

    if apply_final_norm:
        @pl.when(f == pl.num_programs(1) - 1)
        def _():
            o_ref[...] = _rmsnorm(o_ref[...], fg_ref[...], NORM_EPS)


def _ffn(x, g, wg, wu, wd, w_rem, fg, *, layer, n_main, apply_final_norm,
         emit_bf16_weights=False):
    m, d = x.shape
    wg_rem, wu_rem, wd_rem = w_rem
    rem = wg_rem.shape[1]
    tm = min(FFN_TOKEN_TILE, m)
    tf = FFN_EMIT_HIDDEN_TILE if emit_bf16_weights else FFN_HIDDEN_TILE
    assert m % tm == 0 and n_main % tf == 0 and rem % LANES == 0
    assert wg.dtype == (F32 if emit_bf16_weights else BF16)
    assert not emit_bf16_weights or m == tm
    const = lambda i, f: (0, 0)
    resident = functools.partial(pl.BlockSpec, index_map=const, pipeline_mode=pl.Buffered(1))
    col_tile = pl.BlockSpec((None, d, tf), lambda i, f: (layer, 0, f))
    row_tile = pl.BlockSpec((None, tf, d), lambda i, f: (layer, f, 0))
    tok_tile = pl.BlockSpec((tm, d), lambda i, f: (i, 0))
    out_specs, out_shape = [tok_tile], [jax.ShapeDtypeStruct((m, d), F32)]
    if emit_bf16_weights:
        out_specs += [pl.BlockSpec((None, d, tf), lambda i, f: (0, 0, f))] * 2
        out_specs += [pl.BlockSpec((None, tf, d), lambda i, f: (0, f, 0))]
        out_shape += [jax.ShapeDtypeStruct((1, d, n_main), BF16)] * 2
        out_shape += [jax.ShapeDtypeStruct((1, n_main, d), BF16)]
    w_bytes = wg.dtype.itemsize + (2 if emit_bf16_weights else 0)
    vmem = (2 * 2 * tm * d * 4 + tm * d * 2 + 2 * 3 * d * tf * w_bytes + 3 * d * rem * 2
            + 3 * tm * tf * 4)
    outs = pl.pallas_call(
        functools.partial(_ffn_body, apply_final_norm=apply_final_norm,
                          emit_bf16_weights=emit_bf16_weights),
        grid=(m // tm, n_main // tf),
        in_specs=[
            tok_tile,
            pl.BlockSpec((1, d), const),
            col_tile, col_tile, row_tile,
            resident((d, rem)), resident((d, rem)), resident((rem, d)),
            pl.BlockSpec((1, d), const),
        ],
        out_specs=out_specs,
        out_shape=out_shape,
        scratch_shapes=[pltpu.VMEM((tm, d), BF16)],
        compiler_params=pltpu.CompilerParams(
            dimension_semantics=("parallel", "arbitrary"),
            vmem_limit_bytes=_vmem_limit(vmem)),
        name="ffn_half_step",
    )(x, g, wg, wu, wd, wg_rem, wu_rem, wd_rem, fg)
    return outs if emit_bf16_weights else outs[0]


def _ffn_both(xp, xs, g, wg, wu, wd, fg, *, layer, apply_final_norm):
    d_ff = wg.shape[-1]
    n_main = d_ff // FFN_HIDDEN_TILE * FFN_HIDDEN_TILE
    w_rem = (wg[layer, :, n_main:].astype(BF16), wu[layer, :, n_main:].astype(BF16),
             wd[layer, n_main:, :].astype(BF16))
    xs, wg_b, wu_b, wd_b = _ffn(xs, g, wg, wu, wd, w_rem, fg, layer=layer, n_main=n_main,
                                apply_final_norm=apply_final_norm, emit_bf16_weights=True)
    xp = _ffn(xp, g, wg_b, wu_b, wd_b, w_rem, fg, layer=0, n_main=n_main,
              apply_final_norm=apply_final_norm)
    return xp, xs


def _inproj_body(x_ref, g_ref, w_ref, q_ref, k_ref, v_ref, u_ref, kb_ref, vb_ref, *, q_scale):
    h = _rmsnorm(x_ref[...], g_ref[...], NORM_EPS).astype(BF16)
    a = q_ref.shape[1]
    q_ref[...] = (jnp.dot(h, w_ref[:, 0:a], preferred_element_type=F32) * q_scale).astype(BF16)
    k = jnp.dot(h, w_ref[:, a:2 * a], preferred_element_type=F32)
    k_ref[...] = k
    kb_ref[...] = k.astype(BF16)
    v = jnp.dot(h, w_ref[:, 2 * a:3 * a], preferred_element_type=F32)
    v_ref[...] = v
    vb_ref[...] = v.astype(BF16)
    u_ref[...] = jnp.dot(h, w_ref[:, 3 * a:], preferred_element_type=F32)


def _inproj_outputs(m, a, pw):
    return [jax.ShapeDtypeStruct((m, a), BF16), jax.ShapeDtypeStruct((m, a), F32),
            jax.ShapeDtypeStruct((m, a), F32), jax.ShapeDtypeStruct((m, pw), F32),
            jax.ShapeDtypeStruct((m, a), BF16), jax.ShapeDtypeStruct((m, a), BF16)]


def _inproj_converting_body(x_ref, g_ref, w_ref, q_ref, k_ref, v_ref, u_ref, kb_ref, vb_ref,
                            wb_ref, h_ref, *, q_scale):
    j = pl.program_id(0)

    @pl.when(j == 0)
    def _():
        h_ref[...] = _rmsnorm(x_ref[...], g_ref[...], NORM_EPS).astype(BF16)

    w = w_ref[...].astype(BF16)
    wb_ref[...] = w
    y = jnp.dot(h_ref[...], w, preferred_element_type=F32)

    @pl.when(j == 0)
    def _():
        q_ref[...] = (y * q_scale).astype(BF16)

    @pl.when(j == 1)
    def _():
        k_ref[...] = y
        kb_ref[...] = y.astype(BF16)

    @pl.when(j == 2)
    def _():
        v_ref[...] = y
        vb_ref[...] = y.astype(BF16)

    @pl.when(j == 3)
    def _():
        u_ref[...] = y


def _inproj_converting(x, g, w_in, attn_width, *, layer):
    m, d = x.shape
    n = w_in.shape[-1]
    a = attn_width
    head_dim = a // (2 * N_HEADS)
    assert n == 4 * a and m <= PROJ_TOKEN_TILE
    whole = lambda j: (0, 0)
    vmem = (m * d * 4 + 2 * d * a * (4 + 2) + 2 * m * a * (2 + 4 + 4 + 4 + 2 + 2) + m * d * 2
            + m * a * 4)
    *outs, w_bf16 = pl.pallas_call(
        functools.partial(_inproj_converting_body, q_scale=head_dim ** -0.5),
        grid=(4,),
        in_specs=[
            pl.BlockSpec((m, d), whole, pipeline_mode=pl.Buffered(1)),
            pl.BlockSpec((1, d), whole),
            pl.BlockSpec((None, d, a), lambda j: (layer, 0, j)),
        ],
        out_specs=[pl.BlockSpec((m, a), whole)] * 6
                  + [pl.BlockSpec((None, d, a), lambda j: (0, 0, j))],
        out_shape=_inproj_outputs(m, a, a) + [jax.ShapeDtypeStruct((1, d, n), BF16)],
        scratch_shapes=[pltpu.VMEM((m, d), BF16)],
        compiler_params=pltpu.CompilerParams(
            dimension_semantics=("arbitrary",), vmem_limit_bytes=_vmem_limit(vmem)),
        name="input_projection_converting",
    )(x, g, w_in)
    return outs, w_bf16


def _inproj(x, g, w_in, attn_width, *, layer):
    m, d = x.shape
    n = w_in.shape[-1]
    a = attn_width
    pw = n - 3 * a
    head_dim = a // (2 * N_HEADS)
    tm = min(PROJ_TOKEN_TILE, m)
    assert m % tm == 0
    vmem = 2 * tm * d * 4 + d * n * 2 + 2 * tm * (a * 2 + 2 * a * 6 + pw * 4) + tm * n * 4
    row = lambda i: (i, 0)
    return pl.pallas_call(
        functools.partial(_inproj_body, q_scale=head_dim ** -0.5),
        grid=(m // tm,),
        in_specs=[
            pl.BlockSpec((tm, d), row),
            pl.BlockSpec((1, d), lambda i: (0, 0)),
            pl.BlockSpec((None, d, n), lambda i: (layer, 0, 0), pipeline_mode=pl.Buffered(1)),
        ],
        out_specs=[
            pl.BlockSpec((tm, a), row), pl.BlockSpec((tm, a), row), pl.BlockSpec((tm, a), row),
            pl.BlockSpec((tm, pw), row), pl.BlockSpec((tm, a), row), pl.BlockSpec((tm, a), row),
        ],
        out_shape=_inproj_outputs(m, a, pw),
        compiler_params=pltpu.CompilerParams(
            dimension_semantics=("parallel",), vmem_limit_bytes=_vmem_limit(vmem)),
        name="input_projection",
    )(x, g, w_in)


def _subln(o, gain, lam_init):
    return _rmsnorm(o, gain, SUBLN_EPS) * (1.0 - lam_init)


def _prompt_attn_body(lq1_ref, lk1_ref, lq2_ref, lk2_ref, sg_ref, q_ref, k_ref, v_ref, o_ref,
                      k1_ref, k2_ref, vl_ref, vr_ref, *, lam_init):
    seq, vd = q_ref.shape
    hd = vd // 2
    tq = ATTN_Q_TILE
    nt = (((1,), (1,)), ((), ()))
    lam = _diff_lambda(lq1_ref, lk1_ref, lq2_ref, lk2_ref, lam_init)

    k = k_ref[...]
    lane = lax.broadcasted_iota(jnp.int32, (seq, vd), 1)
    k1_ref[...] = jnp.where(lane < hd, k, jnp.zeros_like(k))
    k2_ref[...] = jnp.where(lane >= hd, k, jnp.zeros_like(k))
    v = v_ref[...]
    vl_ref[:, :vd] = v
    vl_ref[:, vd:] = jnp.zeros_like(v)
    vr_ref[:, :vd] = jnp.zeros_like(v)
    vr_ref[:, vd:] = v

    row = lax.broadcasted_iota(jnp.int32, (tq, tq), 0)
    col = lax.broadcasted_iota(jnp.int32, (tq, tq), 1)
    causal = col <= row

    def scores(q, k_half_ref, rows):
        return lax.dot_general(q, k_half_ref[rows, :], nt, preferred_element_type=F32)

    def weighted_values(p1, p2, rows):
        return (jnp.dot(p1.astype(BF16), vl_ref[rows, :], preferred_element_type=F32)
                + jnp.dot(p2.astype(BF16), vr_ref[rows, :], preferred_element_type=F32))

    for i in range(seq // tq):
        diag, past = slice(i * tq, (i + 1) * tq), slice(0, i * tq)
        q = q_ref[diag, :]
        s1d = jnp.where(causal, scores(q, k1_ref, diag), MASK_VALUE)
        s2d = jnp.where(causal, scores(q, k2_ref, diag), MASK_VALUE)
        m1 = jnp.max(s1d, axis=-1, keepdims=True)
        m2 = jnp.max(s2d, axis=-1, keepdims=True)
        if i > 0:
            s1p, s2p = scores(q, k1_ref, past), scores(q, k2_ref, past)
            m1 = jnp.maximum(m1, jnp.max(s1p, axis=-1, keepdims=True))
            m2 = jnp.maximum(m2, jnp.max(s2p, axis=-1, keepdims=True))
        p1d, p2d = jnp.exp(s1d - m1), jnp.exp(s2d - m2)
        l1 = jnp.sum(p1d, axis=-1, keepdims=True)
        l2 = jnp.sum(p2d, axis=-1, keepdims=True)
        acc = weighted_values(p1d, p2d, diag)
        if i > 0:
            p1p, p2p = jnp.exp(s1p - m1), jnp.exp(s2p - m2)
            l1 = l1 + jnp.sum(p1p, axis=-1, keepdims=True)
            l2 = l2 + jnp.sum(p2p, axis=-1, keepdims=True)
            acc = acc + weighted_values(p1p, p2p, past)
        o = acc[:, :vd] * (1.0 / l1) - lam * (acc[:, vd:] * (1.0 / l2))
        o_ref[diag, :] = _subln(o, sg_ref[...], lam_init).astype(o_ref.dtype)


def _prompt_attention(lams, sg, qb, kb, vb, *, batch, seq, lam_init):
    m, a = qb.shape
    vd = a // N_HEADS
    tq = ATTN_Q_TILE
    assert seq % tq == 0 and 2 * vd == V7X_MXU_DIM
    lam_spec = pl.BlockSpec((1, vd // 2), lambda b, h: (0, 0))
    head_spec = pl.BlockSpec((seq, vd), lambda b, h: (b, h))
    vmem = 2 * 4 * seq * vd * 2 + 6 * seq * vd * 2 + 8 * tq * seq * 4
    return pl.pallas_call(
        functools.partial(_prompt_attn_body, lam_init=lam_init),
        grid=(batch, N_HEADS),
        in_specs=[lam_spec] * 4 + [pl.BlockSpec((1, vd), lambda b, h: (0, 0)),
                                   head_spec, head_spec, head_spec],
        out_specs=head_spec,
        out_shape=jax.ShapeDtypeStruct((m, a), BF16),
        scratch_shapes=[pltpu.VMEM((seq, vd), BF16), pltpu.VMEM((seq, vd), BF16),
                        pltpu.VMEM((seq, 2 * vd), BF16), pltpu.VMEM((seq, 2 * vd), BF16)],
        compiler_params=pltpu.CompilerParams(
            dimension_semantics=("parallel", "parallel"),
            vmem_limit_bytes=_vmem_limit(vmem)),
        name="prompt_diff_attention",
    )(*lams, sg, qb, kb, vb)


def _sample_attn_body(pt_ref, lq1_ref, lk1_ref, lq2_ref, lk2_ref, sg_ref, q_ref, kn_ref, vn_ref,
                      *rest, n_pages, lam_init):
    del pt_ref
    k_pages = rest[:n_pages]
    v_pages = rest[n_pages:2 * n_pages]
    o_ref, kbf_ref, vbf_ref = rest[2 * n_pages:]
    page = k_pages[0].shape[0] // N_HEADS
    past = n_pages * page
    n_new, a = q_ref.shape
    vd = a // N_HEADS
    hd = vd // 2
    n_keys = kbf_ref.shape[0]
    pair = 2 * vd
    assert pair == V7X_MXU_DIM and 2 * n_new == 8

    for j in range(n_pages):
        for h in range(N_HEADS):
            rows, cols = slice(j * page, (j + 1) * page), slice(h * vd, (h + 1) * vd)
            head_rows = pl.ds(h, page, stride=N_HEADS)
            kbf_ref[rows, cols] = k_pages[j][head_rows, :].astype(BF16)
            vbf_ref[rows, cols] = v_pages[j][head_rows, :].astype(BF16)
    pad = jnp.zeros((SAMPLE_KEY_PAD - n_new, a), F32)
    kbf_ref[past:, :] = jnp.concatenate([kn_ref[...], pad], axis=0).astype(BF16)
    vbf_ref[past:, :] = jnp.concatenate([vn_ref[...], pad], axis=0).astype(BF16)

    lam = _diff_lambda(lq1_ref, lk1_ref, lq2_ref, lk2_ref, lam_init)
    qf = q_ref[...].astype(F32)
    n_rows = 4 * n_new
    r = lax.broadcasted_iota(jnp.int32, (n_rows, pair), 0)
    c = lax.broadcasted_iota(jnp.int32, (n_rows, pair), 1)
    half, head, tok = r // (2 * n_new), (r // n_new) % 2, r % n_new
    q_mask = (c // hd) == (2 * head + half)
    key = lax.broadcasted_iota(jnp.int32, (n_rows, n_keys), 1)
    tok_k = lax.broadcasted_iota(jnp.int32, (n_rows, n_keys), 0) % n_new
    visible = (key < past) | (key - past <= tok_k)
    out_r = lax.broadcasted_iota(jnp.int32, (2 * n_new, pair), 0) // n_new
    out_c = lax.broadcasted_iota(jnp.int32, (2 * n_new, pair), 1) // vd

    for n in range(a // pair):
        cols = slice(n * pair, (n + 1) * pair)
        qp = qf[:, cols]
        q_bd = jnp.where(q_mask, jnp.concatenate([qp] * 4, axis=0), 0.0).astype(BF16)
        s = lax.dot_general(q_bd, kbf_ref[:, cols], (((1,), (1,)), ((), ())),
                            preferred_element_type=F32)
        s = jnp.where(visible, s, MASK_VALUE)
        p = jnp.exp(s - jnp.max(s, axis=-1, keepdims=True))
        p = p * (1.0 / jnp.sum(p, axis=-1, keepdims=True))
        attn = (p[:2 * n_new] - lam * p[2 * n_new:]).astype(BF16)
        res = jnp.dot(attn, vbf_ref[:, cols], preferred_element_type=F32)
        res = jnp.where(out_r == out_c, res, 0.0)
        o2 = res[:, :vd] + res[:, vd:]
        o2 = _subln(o2, sg_ref[...], lam_init)
        o_ref[:, n * pair:n * pair + vd] = o2[:n_new]
        o_ref[:, n * pair + vd:(n + 1) * pair] = o2[n_new:]


def _sample_attention(page_table, lams, sg, q, k_new, v_new, k_pool, v_pool, *, page, first_page,
                      lam_init):
    nb, n_new, a = q.shape
    n_pages = page_table.shape[1]
    vd = a // N_HEADS
    page_rows = page * N_HEADS
    n_keys = n_pages * page + SAMPLE_KEY_PAD
    const = lambda b, pt: (0, 0)
    tok_spec = pl.BlockSpec((None, n_new, a), lambda b, pt: (b, 0, 0))
    page_specs = [pl.BlockSpec((page_rows, vd), lambda b, pt, j=j: (first_page + pt[b, j], 0))
                  for j in range(n_pages)]
    vmem = 2 * 2 * n_pages * page * a * 4 + 2 * n_keys * a * 2 + 8 * 16 * n_keys * 4
    return pl.pallas_call(
        functools.partial(_sample_attn_body, n_pages=n_pages, lam_init=lam_init),
        grid_spec=pltpu.PrefetchScalarGridSpec(
            num_scalar_prefetch=1,
            grid=(nb,),
            in_specs=[pl.BlockSpec((1, vd // 2), const)] * 4 + [pl.BlockSpec((1, vd), const)]
                     + [tok_spec] * 3 + page_specs + page_specs,
            out_specs=tok_spec,
            scratch_shapes=[pltpu.VMEM((n_keys, a), BF16), pltpu.VMEM((n_keys, a), BF16)],
        ),
        out_shape=jax.ShapeDtypeStruct((nb, n_new, a), F32),
        compiler_params=pltpu.CompilerParams(
            dimension_semantics=("arbitrary",), vmem_limit_bytes=_vmem_limit(vmem)),
        name="sample_diff_attention",
    )(page_table, *lams, sg, q, k_new, v_new, *([k_pool] * n_pages), *([v_pool] * n_pages))


def _pool_project(d, wp_ref, scale_ref, g, cols):
    return jnp.dot(d.astype(BF16), wp_ref[g], preferred_element_type=F32) * scale_ref[:, cols]


def _prompt_pool_tile(i, u_ref, halo_ref, wp_ref, scale_ref, z_ref, ext_ref):
    tm, c = u_ref.shape
    cg = c // len(POOL_WINDOWS)
    ext_ref[0:POOL_HALO, :] = jnp.where(i == 0, 0.0, halo_ref[...])
    ext_ref[POOL_HALO:, :] = u_ref[...]
    pos = i * tm + lax.broadcasted_iota(jnp.int32, (tm, 1), 0)
    for g, w in enumerate(POOL_WINDOWS):
        cols = slice(g * cg, (g + 1) * cg)
        cur = ext_ref[POOL_HALO:, cols]
        win = cur
        for s in range(1, w):
            win = win + ext_ref[POOL_HALO - s:POOL_HALO - s + tm, cols]
        cnt = jnp.minimum(w, pos + 1).astype(F32)
        d = win * (1.0 / cnt) - cur
        z_ref[:, cols] = _pool_project(d, wp_ref, scale_ref, g, cols).astype(z_ref.dtype)


def _sample_pool_body(state_ref, u_ref, wp_ref, scale_ref, z_ref, *, pos0):
    n_state = state_ref.shape[0]
    n_new, _, c = u_ref.shape
    cg = c // len(POOL_WINDOWS)

    def ext(r, cols):
        return state_ref[r, :, cols] if r < n_state else u_ref[r - n_state, :, cols]

    for g, w in enumerate(POOL_WINDOWS):
        cols = slice(g * cg, (g + 1) * cg)
        for t in range(n_new):
            cur = ext(n_state + t, cols)
            win = cur
            for s in range(1, w):
                win = win + ext(n_state + t - s, cols)
            d = win * (1.0 / min(w, pos0 + t + 1)) - cur
            z_ref[t, :, cols] = _pool_project(d, wp_ref, scale_ref, g, cols).astype(z_ref.dtype)


def _sample_pool(state_t, u_t, wp, scale, *, pos0):
    n_new, nb, c = u_t.shape
    vmem = 2 * (state_t.size + u_t.size) * 4 + wp.size * 4 + 4 * nb * c * 4
    return pl.pallas_call(
        functools.partial(_sample_pool_body, pos0=pos0),
        out_shape=jax.ShapeDtypeStruct((n_new, nb, c), BF16),
        compiler_params=pltpu.CompilerParams(vmem_limit_bytes=_vmem_limit(vmem)),
        name="sample_pool_mixer",
    )(state_t, u_t, wp, scale)


def _outproj_body(x_ref, o_ref, z_ref, w_ref, y_ref):
    a = o_ref.shape[1]
    y = x_ref[...] + jnp.dot(o_ref[...].astype(BF16), w_ref[0:a, :], preferred_element_type=F32)
    y_ref[...] = y + jnp.dot(z_ref[...].astype(BF16), w_ref[a:, :], preferred_element_type=F32)


def _outproj(x, o, z, w_out):
    m, d = x.shape
    a, c = o.shape[1], z.shape[1]
    tm = min(PROJ_TOKEN_TILE, m)
    assert m % tm == 0
    row = lambda i: (i, 0)
    vmem = 2 * 2 * tm * d * 4 + 2 * tm * (a + c) * 4 + w_out.size * 2 + tm * d * 4
    return pl.pallas_call(
        _outproj_body,
        grid=(m // tm,),
        in_specs=[
            pl.BlockSpec((tm, d), row), pl.BlockSpec((tm, a), row), pl.BlockSpec((tm, c), row),
            pl.BlockSpec(w_out.shape, lambda i: (0, 0), pipeline_mode=pl.Buffered(1)),
        ],
        out_specs=pl.BlockSpec((tm, d), row),
        out_shape=jax.ShapeDtypeStruct((m, d), F32),
        compiler_params=pltpu.CompilerParams(
            dimension_semantics=("parallel",), vmem_limit_bytes=_vmem_limit(vmem)),
        name="output_projection",
    )(x, o, z, w_out)


def _prompt_outproj_body(x_ref, o_ref, u_ref, halo_ref, wp_ref, scale_ref, w_ref, y_ref,
                         ext_ref, z_ref):
    a = o_ref.shape[1]
    y = x_ref[...] + jnp.dot(o_ref[...], w_ref[0:a, :], preferred_element_type=F32)
    _prompt_pool_tile(pl.program_id(1), u_ref, halo_ref, wp_ref, scale_ref, z_ref, ext_ref)
    y_ref[...] = y + jnp.dot(z_ref[...], w_ref[a:, :], preferred_element_type=F32)


def _prompt_outproj(x, o, u, wp, scale, w_out, *, batch, seq):
    m, d = x.shape
    a, c = o.shape[1], u.shape[1]
    tm = POOL_TOKEN_TILE
    nt = seq // tm
    halo_per_tile = tm // POOL_HALO
    assert seq % tm == 0 and o.dtype == BF16
    row = lambda b, i: (b * nt + i, 0)
    const = lambda b, i: (0, 0)
    vmem = (2 * 2 * tm * d * 4 + 2 * tm * (a * 2 + c * 4) + w_out.size * 2 + wp.size * 2 * 2
            + (tm + POOL_HALO) * c * 4 + tm * c * 2 + tm * d * 4 + 4 * tm * c * 4)
    return pl.pallas_call(
        _prompt_outproj_body,
        grid=(batch, nt),
        in_specs=[
            pl.BlockSpec((tm, d), row), pl.BlockSpec((tm, a), row), pl.BlockSpec((tm, c), row),
            pl.BlockSpec((POOL_HALO, c),
                         lambda b, i: (jnp.maximum((b * nt + i) * halo_per_tile - 1, 0), 0)),
            pl.BlockSpec(wp.shape, lambda b, i: (0, 0, 0)),
            pl.BlockSpec((1, c), const),
            pl.BlockSpec(w_out.shape, const, pipeline_mode=pl.Buffered(1)),
        ],
        out_specs=pl.BlockSpec((tm, d), row),
        out_shape=jax.ShapeDtypeStruct((m, d), F32),
        scratch_shapes=[pltpu.VMEM((tm + POOL_HALO, c), F32), pltpu.VMEM((tm, c), BF16)],
        compiler_params=pltpu.CompilerParams(
            dimension_semantics=("parallel", "arbitrary"), vmem_limit_bytes=_vmem_limit(vmem)),
        name="prompt_pool_output_projection",
    )(x, o, u, u, wp, scale, w_out)


def kernel(x_prompt, x_sample, cache_k, cache_v, state_pool, page_table, ffn1_norm, ffn1_w_gate, ffn1_w_up, ffn1_w_down, mix_norm, w_in, lambda_q1, lambda_k1, lambda_q2, lambda_k2, subln_gain, w_pool, pool_scale, w_out, ffn2_norm, ffn2_w_gate, ffn2_w_up, ffn2_w_down, final_norm):
    batch, seq, d = x_prompt.shape
    nb, n_new, _ = x_sample.shape
    depth, n_phys, page, n_heads, vd = cache_k.shape
    assert n_heads == N_HEADS
    a = n_heads * vd
    c = state_pool.shape[-1]
    past_len = page_table.shape[1] * page
    k_pool = cache_k.reshape(depth * n_phys * page * n_heads, vd)
    v_pool = cache_v.reshape(depth * n_phys * page * n_heads, vd)

    xp = x_prompt.reshape(batch * seq, d)
    xs = x_sample.reshape(nb * n_new, d)
    fg = final_norm.reshape(1, d)
    k_p, v_p, buf_p, k_s, v_s, buf_s = [], [], [], [], [], []
    for l in range(depth):
        lam_init = _lambda_init(l)
        last = l == depth - 1
        mix_g = mix_norm[l].reshape(1, d)
        w_out_b = w_out[l].astype(BF16)
        wp_b = w_pool[l].astype(BF16)
        scale = pool_scale[l].reshape(1, c)
        lams = tuple(p[l].reshape(1, -1) for p in (lambda_q1, lambda_k1, lambda_q2, lambda_k2))
        sg = subln_gain[l].reshape(1, vd)

        xp, xs = _ffn_both(xp, xs, ffn1_norm[l].reshape(1, d), ffn1_w_gate, ffn1_w_up, ffn1_w_down,
                           fg, layer=l, apply_final_norm=False)

        (qs, ks, vs, us, _, _), w_in_b = _inproj_converting(xs, mix_g, w_in, a, layer=l)

        qb, kp, vp, up, kb, vb = _inproj(xp, mix_g, w_in_b, a, layer=0)
        op = _prompt_attention(lams, sg, qb, kb, vb, batch=batch, seq=seq, lam_init=lam_init)
        xp = _prompt_outproj(xp, op, up, wp_b, scale, w_out_b, batch=batch, seq=seq)

        os_ = _sample_attention(
            page_table, lams, sg, qs.reshape(nb, n_new, a), ks.reshape(nb, n_new, a),
            vs.reshape(nb, n_new, a), k_pool, v_pool, page=page,
            first_page=l * n_phys, lam_init=lam_init)
        us3 = us.reshape(nb, n_new, c)
        zs_t = _sample_pool(jnp.transpose(state_pool[l], (1, 0, 2)), jnp.transpose(us3, (1, 0, 2)),
                            wp_b, scale, pos0=past_len)
        zs = jnp.transpose(zs_t, (1, 0, 2)).reshape(nb * n_new, c)
        xs = _outproj(xs, os_.reshape(nb * n_new, a), zs, w_out_b)

        xp, xs = _ffn_both(xp, xs, ffn2_norm[l].reshape(1, d), ffn2_w_gate, ffn2_w_up, ffn2_w_down,
                           fg, layer=l, apply_final_norm=last)

        k_p.append(kp.reshape(batch, seq, n_heads, vd))
        v_p.append(vp.reshape(batch, seq, n_heads, vd))
        buf_p.append(up.reshape(batch, seq, c)[:, seq - POOL_BUF:])
        k_s.append(ks.reshape(nb, n_new, n_heads, vd))
        v_s.append(vs.reshape(nb, n_new, n_heads, vd))
        buf_s.append(jnp.concatenate([state_pool[l], us3], axis=1)[:, -POOL_BUF:])

    return (xp.reshape(batch, seq, d), xs.reshape(nb, n_new, d),
            jnp.stack(k_p), jnp.stack(v_p), jnp.stack(buf_p),
            jnp.stack(k_s), jnp.stack(v_s), jnp.stack(buf_s))
```

```python
import functools
import math

import jax
import jax.numpy as jnp
from jax import lax
from jax.experimental import pallas as pl
from jax.experimental.pallas import tpu as pltpu

F32 = jnp.float32
BF16 = jnp.bfloat16

N_HEADS = 8
POOL_WINDOWS = (2, 4, 8, 16)
POOL_BUF = max(POOL_WINDOWS) - 1
FFN_RESIDUAL = 0.5
NORM_EPS = 1e-6
SUBLN_EPS = 1e-5
MASK_VALUE = -1e30

V7X_VMEM_BYTES = 64 * 1024 * 1024
LANES = 128

FFN_TOKEN_TILE = 512
FFN_HIDDEN_TILE = 1024
FFN_EMIT_HIDDEN_TILE = 256
PROJ_TOKEN_TILE = 512
ATTN_Q_TILE = 256
POOL_TOKEN_TILE = 512
POOL_HALO = 16
SAMPLE_KEY_PAD = 128


def _vmem_limit(nbytes):
    return int(min(nbytes * 5 // 4 + (8 << 20), V7X_VMEM_BYTES - (4 << 20)))


def _rmsnorm(x, g, eps):
    return x * lax.rsqrt(jnp.mean(x * x, axis=-1, keepdims=True) + eps) * g


def _lambda_init(layer):
    return 0.8 - 0.6 * math.exp(-0.3 * layer)


def _diff_lambda(lq1_ref, lk1_ref, lq2_ref, lk2_ref, lam_init):
    a = jnp.sum(lq1_ref[...] * lk1_ref[...], axis=-1, keepdims=True)
    b = jnp.sum(lq2_ref[...] * lk2_ref[...], axis=-1, keepdims=True)
    return jnp.exp(a) - jnp.exp(b) + lam_init


def _swiglu_partial(h, wg, wu, wd):
    gate = jnp.dot(h, wg, preferred_element_type=F32)
    up = jnp.dot(h, wu, preferred_element_type=F32)
    act = (FFN_RESIDUAL * (gate * jax.nn.sigmoid(gate)) * up).astype(BF16)
    return jnp.dot(act, wd, preferred_element_type=F32)


def _ffn_body(x_ref, g_ref, wg_ref, wu_ref, wd_ref, wg_rem_ref, wu_rem_ref, wd_rem_ref, fg_ref,
              o_ref, *rest, apply_final_norm, emit_bf16_weights):
    f = pl.program_id(1)
    h_ref = rest[-1]

    @pl.when(f == 0)
    def _():
        x = x_ref[...]
        h = _rmsnorm(x, g_ref[...], NORM_EPS).astype(BF16)
        h_ref[...] = h
        o_ref[...] = x + _swiglu_partial(h, wg_rem_ref[...], wu_rem_ref[...], wd_rem_ref[...])

    if emit_bf16_weights:
        wg, wu, wd = (r[...].astype(BF16) for r in (wg_ref, wu_ref, wd_ref))
        for out_ref, w in zip(rest[:3], (wg, wu, wd)):
            out_ref[...] = w
    else:
        wg, wu, wd = wg_ref[...], wu_ref[...], wd_ref[...]
    o_ref[...] += _swiglu_partial(h_ref[...], wg, wu, wd)

    if apply_final_norm:
        @pl.when(f == pl.num_programs(1) - 1)
        def _():
            o_ref[...] = _rmsnorm(o_ref[...], fg_ref[...], NORM_EPS)


def _ffn(x, g, wg, wu, wd, w_rem, fg, *, layer, n_main, apply_final_norm,
         emit_bf16_weights=False):
    m, d = x.shape
    wg_rem, wu_rem, wd_rem = w_rem
    rem = wg_rem.shape[1]
    tm = min(FFN_TOKEN_TILE, m)
    tf = FFN_EMIT_HIDDEN_TILE if emit_bf16_weights else FFN_HIDDEN_TILE
    assert m % tm == 0 and n_main % tf == 0 and rem % LANES == 0
    assert wg.dtype == (F32 if emit_bf16_weights else BF16)
    assert not emit_bf16_weights or m == tm
    const = lambda i, f: (0, 0)
    resident = functools.partial(pl.BlockSpec, index_map=const, pipeline_mode=pl.Buffered(1))
    col_tile = pl.BlockSpec((None, d, tf), lambda i, f: (layer, 0, f))
    row_tile = pl.BlockSpec((None, tf, d), lambda i, f: (layer, f, 0))
    tok_tile = pl.BlockSpec((tm, d), lambda i, f: (i, 0))
    out_specs, out_shape = [tok_tile], [jax.ShapeDtypeStruct((m, d), F32)]
    if emit_bf16_weights:
        out_specs += [pl.BlockSpec((None, d, tf), lambda i, f: (0, 0, f))] * 2
        out_specs += [pl.BlockSpec((None, tf, d), lambda i, f: (0, f, 0))]
        out_shape += [jax.ShapeDtypeStruct((1, d, n_main), BF16)] * 2
        out_shape += [jax.ShapeDtypeStruct((1, n_main, d), BF16)]
    w_bytes = wg.dtype.itemsize + (2 if emit_bf16_weights else 0)
    vmem = (2 * 2 * tm * d * 4 + tm * d * 2 + 2 * 3 * d * tf * w_bytes + 3 * d * rem * 2
            + 3 * tm * tf * 4)
    outs = pl.pallas_call(
        functools.partial(_ffn_body, apply_final_norm=apply_final_norm,
                          emit_bf16_weights=emit_bf16_weights),
        grid=(m // tm, n_main // tf),
        in_specs=[
            tok_tile,
            pl.BlockSpec((1, d), const),
            col_tile, col_tile, row_tile,
            resident((d, rem)), resident((d, rem)), resident((rem, d)),
            pl.BlockSpec((1, d), const),
        ],
        out_specs=out_specs,
        out_shape=out_shape,
        scratch_shapes=[pltpu.VMEM((tm, d), BF16)],
        compiler_params=pltpu.CompilerParams(
            dimension_semantics=("parallel", "arbitrary"),
            vmem_limit_bytes=_vmem_limit(vmem)),
        name="ffn_half_step",
    )(x, g, wg, wu, wd, wg_rem, wu_rem, wd_rem, fg)
    return outs if emit_bf16_weights else outs[0]


def _ffn_both(xp, xs, g, wg, wu, wd, fg, *, layer, apply_final_norm):
    d_ff = wg.shape[-1]
    n_main = d_ff // FFN_HIDDEN_TILE * FFN_HIDDEN_TILE
    w_rem = (wg[layer, :, n_main:].astype(BF16), wu[layer, :, n_main:].astype(BF16),
             wd[layer, n_main:, :].astype(BF16))
    xs, wg_b, wu_b, wd_b = _ffn(xs, g, wg, wu, wd, w_rem, fg, layer=layer, n_main=n_main,
                                apply_final_norm=apply_final_norm, emit_bf16_weights=True)
    xp = _ffn(xp, g, wg_b, wu_b, wd_b, w_rem, fg, layer=0, n_main=n_main,
              apply_final_norm=apply_final_norm)
    return xp, xs


def _inproj_body(x_ref, g_ref, w_ref, q_ref, k_ref, v_ref, u_ref, kb_ref, vb_ref, *, q_scale):
    h = _rmsnorm(x_ref[...], g_ref[...], NORM_EPS).astype(BF16)
    a = q_ref.shape[1]
    q_ref[...] = (jnp.dot(h, w_ref[:, 0:a], preferred_element_type=F32) * q_scale).astype(BF16)
    k = jnp.dot(h, w_ref[:, a:2 * a], preferred_element_type=F32)
    k_ref[...] = k
    kb_ref[...] = k.astype(BF16)
    v = jnp.dot(h, w_ref[:, 2 * a:3 * a], preferred_element_type=F32)
    v_ref[...] = v
    vb_ref[...] = v.astype(BF16)
    u_ref[...] = jnp.dot(h, w_ref[:, 3 * a:], preferred_element_type=F32)


def _inproj_outputs(m, a, pw):
    return [jax.ShapeDtypeStruct((m, a), BF16), jax.ShapeDtypeStruct((m, a), F32),
            jax.ShapeDtypeStruct((m, a), F32), jax.ShapeDtypeStruct((m, pw), F32),
            jax.ShapeDtypeStruct((m, a), BF16), jax.ShapeDtypeStruct((m, a), BF16)]


def _inproj_converting_body(x_ref, g_ref, w_ref, q_ref, k_ref, v_ref, u_ref, kb_ref, vb_ref,
                            wb_ref, h_ref, *, q_scale):
    j = pl.program_id(0)

    @pl.when(j == 0)
    def _():
        h_ref[...] = _rmsnorm(x_ref[...], g_ref[...], NORM_EPS).astype(BF16)

    w = w_ref[...].astype(BF16)
    wb_ref[...] = w
    y = jnp.dot(h_ref[...], w, preferred_element_type=F32)

    @pl.when(j == 0)
    def _():
        q_ref[...] = (y * q_scale).astype(BF16)

    @pl.when(j == 1)
    def _():
        k_ref[...] = y
        kb_ref[...] = y.astype(BF16)

    @pl.when(j == 2)
    def _():
        v_ref[...] = y
        vb_ref[...] = y.astype(BF16)

    @pl.when(j == 3)
    def _():
        u_ref[...] = y


def _inproj_converting(x, g, w_in, attn_width, *, layer):
    m, d = x.shape
    n = w_in.shape[-1]
    a = attn_width
    head_dim = a // (2 * N_HEADS)
    assert n == 4 * a and m <= PROJ_TOKEN_TILE
    whole = lambda j: (0, 0)
    vmem = (m * d * 4 + 2 * d * a * (4 + 2) + 2 * m * a * (2 + 4 + 4 + 4 + 2 + 2) + m * d * 2
            + m * a * 4)
    *outs, w_bf16 = pl.pallas_call(
        functools.partial(_inproj_converting_body, q_scale=head_dim ** -0.5),
        grid=(4,),
        in_specs=[
            pl.BlockSpec((m, d), whole, pipeline_mode=pl.Buffered(1)),
            pl.BlockSpec((1, d), whole),
            pl.BlockSpec((None, d, a), lambda j: (layer, 0, j)),
        ],
        out_specs=[pl.BlockSpec((m, a), whole)] * 6
                  + [pl.BlockSpec((None, d, a), lambda j: (0, 0, j))],
        out_shape=_inproj_outputs(m, a, a) + [jax.ShapeDtypeStruct((1, d, n), BF16)],
        scratch_shapes=[pltpu.VMEM((m, d), BF16)],
        compiler_params=pltpu.CompilerParams(
            dimension_semantics=("arbitrary",), vmem_limit_bytes=_vmem_limit(vmem)),
        name="input_projection_converting",
    )(x, g, w_in)
    return outs, w_bf16


def _inproj(x, g, w_in, attn_width, *, layer):
    m, d = x.shape
    n = w_in.shape[-1]
    a = attn_width
    pw = n - 3 * a
    head_dim = a // (2 * N_HEADS)
    tm = min(PROJ_TOKEN_TILE, m)
    assert m % tm == 0
    vmem = 2 * tm * d * 4 + d * n * 2 + 2 * tm * (a * 2 + 2 * a * 6 + pw * 4) + tm * n * 4
    row = lambda i: (i, 0)
    return pl.pallas_call(
        functools.partial(_inproj_body, q_scale=head_dim ** -0.5),
        grid=(m // tm,),
        in_specs=[
            pl.BlockSpec((tm, d), row),
            pl.BlockSpec((1, d), lambda i: (0, 0)),
            pl.BlockSpec((None, d, n), lambda i: (layer, 0, 0), pipeline_mode=pl.Buffered(1)),
        ],
        out_specs=[
            pl.BlockSpec((tm, a), row), pl.BlockSpec((tm, a), row), pl.BlockSpec((tm, a), row),
            pl.BlockSpec((tm, pw), row), pl.BlockSpec((tm, a), row), pl.BlockSpec((tm, a), row),
        ],
        out_shape=_inproj_outputs(m, a, pw),
        compiler_params=pltpu.CompilerParams(
            dimension_semantics=("parallel",), vmem_limit_bytes=_vmem_limit(vmem)),
        name="input_projection",
    )(x, g, w_in)


def _subln(o, gain, lam_init):
    return _rmsnorm(o, gain, SUBLN_EPS) * (1.0 - lam_init)


def _prompt_attn_body(lq1_ref, lk1_ref, lq2_ref, lk2_ref, sg_ref, q_ref, k_ref, v_ref, o_ref, *,
                      lam_init):
    seq, vd = q_ref.shape
    hd = vd // 2
    tq = ATTN_Q_TILE
    nt = (((1,), (1,)), ((), ()))
    lam = _diff_lambda(lq1_ref, lk1_ref, lq2_ref, lk2_ref, lam_init)
    lane = lax.broadcasted_iota(jnp.int32, (tq, vd), 1)
    row = lax.broadcasted_iota(jnp.int32, (2 * tq, tq), 0)
    row = jnp.where(row >= tq, row - tq, row)
    col = lax.broadcasted_iota(jnp.int32, (2 * tq, tq), 1)
    causal = col <= row

    for i in range(seq // tq):
        lo, hi = i * tq, (i + 1) * tq
        q = q_ref[lo:hi, :]
        zero = jnp.zeros_like(q)
        qz = jnp.concatenate([jnp.where(lane < hd, q, zero), jnp.where(lane >= hd, q, zero)],
                             axis=0)
        s_diag = lax.dot_general(qz, k_ref[lo:hi, :], nt, preferred_element_type=F32)
        s_diag = jnp.where(causal, s_diag, MASK_VALUE)
        m = jnp.max(s_diag, axis=-1, keepdims=True)
        if i > 0:
            s_past = lax.dot_general(qz, k_ref[0:lo, :], nt, preferred_element_type=F32)
            m = jnp.maximum(m, jnp.max(s_past, axis=-1, keepdims=True))
        p_diag = jnp.exp(s_diag - m)
        l = jnp.sum(p_diag, axis=-1, keepdims=True)
        acc = jnp.dot(p_diag.astype(BF16), v_ref[lo:hi, :], preferred_element_type=F32)
        if i > 0:
            p_past = jnp.exp(s_past - m)
            l = l + jnp.sum(p_past, axis=-1, keepdims=True)
            acc = acc + jnp.dot(p_past.astype(BF16), v_ref[0:lo, :], preferred_element_type=F32)
        on = acc * (1.0 / l)
        o = on[:tq] - lam * on[tq:]
        o_ref[lo:hi, :] = _subln(o, sg_ref[...], lam_init).astype(o_ref.dtype)


def _prompt_attention(lams, sg, qb, kb, vb, *, batch, seq, lam_init):
    m, a = qb.shape
    vd = a // N_HEADS
    tq = ATTN_Q_TILE
    assert seq % tq == 0
    lam_spec = pl.BlockSpec((1, vd // 2), lambda b, h: (0, 0))
    head_spec = pl.BlockSpec((seq, vd), lambda b, h: (b, h))
    vmem = 2 * 4 * seq * vd * 2 + 4 * 2 * tq * seq * 4
    return pl.pallas_call(
        functools.partial(_prompt_attn_body, lam_init=lam_init),
        grid=(batch, N_HEADS),
        in_specs=[lam_spec] * 4 + [pl.BlockSpec((1, vd), lambda b, h: (0, 0)),
                                   head_spec, head_spec, head_spec],
        out_specs=head_spec,
        out_shape=jax.ShapeDtypeStruct((m, a), BF16),
        compiler_params=pltpu.CompilerParams(
            dimension_semantics=("parallel", "parallel"),
            vmem_limit_bytes=_vmem_limit(vmem)),
        name="prompt_diff_attention",
    )(*lams, sg, qb, kb, vb)


def _sample_attn_body(pt_ref, lq1_ref, lk1_ref, lq2_ref, lk2_ref, sg_ref, q_ref, kn_ref, vn_ref,
                      *rest, n_pages, lam_init):
    del pt_ref
    k_pages = rest[:n_pages]
    v_pages = rest[n_pages:2 * n_pages]
    o_ref, kbf_ref, vbf_ref = rest[2 * n_pages:]
    page_rows = k_pages[0].shape[0]
    past_rows = n_pages * page_rows
    n_rows_k = kbf_ref.shape[0]
    n_new, a = q_ref.shape
    vd = a // N_HEADS
    hd = vd // 2
    head_cols = [slice(h * vd, (h + 1) * vd) for h in range(N_HEADS)]

    for j in range(n_pages):
        rows = slice(j * page_rows, (j + 1) * page_rows)
        kbf_ref[rows, :] = k_pages[j][...].astype(BF16)
        vbf_ref[rows, :] = v_pages[j][...].astype(BF16)
    kn, vn = kn_ref[...], vn_ref[...]
    pad = jnp.zeros((n_rows_k - past_rows - N_HEADS * n_new, vd), F32)
    kbf_ref[past_rows:, :] = jnp.concatenate([kn[:, hc] for hc in head_cols] + [pad],
                                             axis=0).astype(BF16)
    vbf_ref[past_rows:, :] = jnp.concatenate([vn[:, hc] for hc in head_cols] + [pad],
                                             axis=0).astype(BF16)

    lam = _diff_lambda(lq1_ref, lk1_ref, lq2_ref, lk2_ref, lam_init)
    qf = q_ref[...].astype(F32)
    n_q = 2 * N_HEADS * n_new
    r = lax.broadcasted_iota(jnp.int32, (n_q, vd), 0)
    c = lax.broadcasted_iota(jnp.int32, (n_q, vd), 1)
    q2 = jnp.concatenate([qf[:, hc] for hc in head_cols] * 2, axis=0)
    q2 = jnp.where((c // hd) == (r // (N_HEADS * n_new)), q2, 0.0).astype(BF16)

    col = lax.broadcasted_iota(jnp.int32, (n_q, n_rows_k), 1)
    row = lax.broadcasted_iota(jnp.int32, (n_q, n_rows_k), 0)
    q_head, q_tok = (row // n_new) % N_HEADS, row % n_new
    new_col = col - past_rows
    visible = (((col < past_rows) & ((col % N_HEADS) == q_head))
               | ((new_col >= 0) & (new_col < N_HEADS * n_new) & ((new_col // n_new) == q_head)
                  & ((new_col % n_new) <= q_tok)))

    s = lax.dot_general(q2, kbf_ref[...], (((1,), (1,)), ((), ())),
                        preferred_element_type=F32)
    s = jnp.where(visible, s, MASK_VALUE)
    p = jnp.exp(s - jnp.max(s, axis=-1, keepdims=True))
    p = p * (1.0 / jnp.sum(p, axis=-1, keepdims=True))
    attn = (p[:n_q // 2] - lam * p[n_q // 2:]).astype(BF16)
    o = jnp.dot(attn, vbf_ref[...], preferred_element_type=F32)
    o = _subln(o, sg_ref[...], lam_init)
    for h, hc in enumerate(head_cols):
        o_ref[:, hc] = o[h * n_new:(h + 1) * n_new]


def _sample_attention(page_table, lams, sg, q, k_new, v_new, k_pool, v_pool, *, page, first_page,
                      lam_init):
    nb, n_new, a = q.shape
    n_pages = page_table.shape[1]
    vd = a // N_HEADS
    page_rows = page * N_HEADS
    n_keys = n_pages * page + SAMPLE_KEY_PAD
    const = lambda b, pt: (0, 0)
    tok_spec = pl.BlockSpec((None, n_new, a), lambda b, pt: (b, 0, 0))
    page_specs = [pl.BlockSpec((page_rows, vd), lambda b, pt, j=j: (first_page + pt[b, j], 0))
                  for j in range(n_pages)]
    vmem = 2 * 2 * n_pages * page * a * 4 + 2 * n_keys * a * 2 + 8 * 16 * n_keys * 4
    return pl.pallas_call(
        functools.partial(_sample_attn_body, n_pages=n_pages, lam_init=lam_init),
        grid_spec=pltpu.PrefetchScalarGridSpec(
            num_scalar_prefetch=1,
            grid=(nb,),
            in_specs=[pl.BlockSpec((1, vd // 2), const)] * 4 + [pl.BlockSpec((1, vd), const)]
                     + [tok_spec] * 3 + page_specs + page_specs,
            out_specs=tok_spec,
            scratch_shapes=[pltpu.VMEM((n_keys * N_HEADS, vd), BF16)] * 2,
        ),
        out_shape=jax.ShapeDtypeStruct((nb, n_new, a), F32),
        compiler_params=pltpu.CompilerParams(
            dimension_semantics=("arbitrary",), vmem_limit_bytes=_vmem_limit(vmem)),
        name="sample_diff_attention",
    )(page_table, *lams, sg, q, k_new, v_new, *([k_pool] * n_pages), *([v_pool] * n_pages))


def _pool_project(d, wp_ref, scale_ref, g, cols):
    return jnp.dot(d.astype(BF16), wp_ref[g], preferred_element_type=F32) * scale_ref[:, cols]


def _prompt_pool_tile(i, u_ref, halo_ref, wp_ref, scale_ref, z_ref, ext_ref):
    tm, c = u_ref.shape
    cg = c // len(POOL_WINDOWS)
    ext_ref[0:POOL_HALO, :] = jnp.where(i == 0, 0.0, halo_ref[...])
    ext_ref[POOL_HALO:, :] = u_ref[...]
    pos = i * tm + lax.broadcasted_iota(jnp.int32, (tm, 1), 0)
    for g, w in enumerate(POOL_WINDOWS):
        cols = slice(g * cg, (g + 1) * cg)
        cur = ext_ref[POOL_HALO:, cols]
        win = cur
        for s in range(1, w):
            win = win + ext_ref[POOL_HALO - s:POOL_HALO - s + tm, cols]
        cnt = jnp.minimum(w, pos + 1).astype(F32)
        d = win * (1.0 / cnt) - cur
        z_ref[:, cols] = _pool_project(d, wp_ref, scale_ref, g, cols).astype(z_ref.dtype)


def _sample_pool_body(state_ref, u_ref, wp_ref, scale_ref, z_ref, *, pos0):
    n_state = state_ref.shape[0]
    n_new, _, c = u_ref.shape
    cg = c // len(POOL_WINDOWS)

    def ext(r, cols):
        return state_ref[r, :, cols] if r < n_state else u_ref[r - n_state, :, cols]

    for g, w in enumerate(POOL_WINDOWS):
        cols = slice(g * cg, (g + 1) * cg)
        for t in range(n_new):
            cur = ext(n_state + t, cols)
            win = cur
            for s in range(1, w):
                win = win + ext(n_state + t - s, cols)
            d = win * (1.0 / min(w, pos0 + t + 1)) - cur
            z_ref[t, :, cols] = _pool_project(d, wp_ref, scale_ref, g, cols).astype(z_ref.dtype)


def _sample_pool(state_t, u_t, wp, scale, *, pos0):
    n_new, nb, c = u_t.shape
    vmem = 2 * (state_t.size + u_t.size) * 4 + wp.size * 4 + 4 * nb * c * 4
    return pl.pallas_call(
        functools.partial(_sample_pool_body, pos0=pos0),
        out_shape=jax.ShapeDtypeStruct((n_new, nb, c), BF16),
        compiler_params=pltpu.CompilerParams(vmem_limit_bytes=_vmem_limit(vmem)),
        name="sample_pool_mixer",
    )(state_t, u_t, wp, scale)


def _outproj_body(x_ref, o_ref, z_ref, w_ref, y_ref):
    a = o_ref.shape[1]
    y = x_ref[...] + jnp.dot(o_ref[...].astype(BF16), w_ref[0:a, :], preferred_element_type=F32)
    y_ref[...] = y + jnp.dot(z_ref[...].astype(BF16), w_ref[a:, :], preferred_element_type=F32)


def _outproj(x, o, z, w_out):
    m, d = x.shape
    a, c = o.shape[1], z.shape[1]
    tm = min(PROJ_TOKEN_TILE, m)
    assert m % tm == 0
    row = lambda i: (i, 0)
    vmem = 2 * 2 * tm * d * 4 + 2 * tm * (a + c) * 4 + w_out.size * 2 + tm * d * 4
    return pl.pallas_call(
        _outproj_body,
        grid=(m // tm,),
        in_specs=[
            pl.BlockSpec((tm, d), row), pl.BlockSpec((tm, a), row), pl.BlockSpec((tm, c), row),
            pl.BlockSpec(w_out.shape, lambda i: (0, 0), pipeline_mode=pl.Buffered(1)),
        ],
        out_specs=pl.BlockSpec((tm, d), row),
        out_shape=jax.ShapeDtypeStruct((m, d), F32),
        compiler_params=pltpu.CompilerParams(
            dimension_semantics=("parallel",), vmem_limit_bytes=_vmem_limit(vmem)),
        name="output_projection",
    )(x, o, z, w_out)


def _prompt_outproj_body(x_ref, o_ref, u_ref, halo_ref, wp_ref, scale_ref, w_ref, y_ref,
                         ext_ref, z_ref):
    a = o_ref.shape[1]
    y = x_ref[...] + jnp.dot(o_ref[...], w_ref[0:a, :], preferred_element_type=F32)
    _prompt_pool_tile(pl.program_id(1), u_ref, halo_ref, wp_ref, scale_ref, z_ref, ext_ref)
    y_ref[...] = y + jnp.dot(z_ref[...], w_ref[a:, :], preferred_element_type=F32)


def _prompt_outproj(x, o, u, wp, scale, w_out, *, batch, seq):
    m, d = x.shape
    a, c = o.shape[1], u.shape[1]
    tm = POOL_TOKEN_TILE
    nt = seq // tm
    halo_per_tile = tm // POOL_HALO
    assert seq % tm == 0 and o.dtype == BF16
    row = lambda b, i: (b * nt + i, 0)
    const = lambda b, i: (0, 0)
    vmem = (2 * 2 * tm * d * 4 + 2 * tm * (a * 2 + c * 4) + w_out.size * 2 + wp.size * 2 * 2
            + (tm + POOL_HALO) * c * 4 + tm * c * 2 + tm * d * 4 + 4 * tm * c * 4)
    return pl.pallas_call(
        _prompt_outproj_body,
        grid=(batch, nt),
        in_specs=[
            pl.BlockSpec((tm, d), row), pl.BlockSpec((tm, a), row), pl.BlockSpec((tm, c), row),
            pl.BlockSpec((POOL_HALO, c),
                         lambda b, i: (jnp.maximum((b * nt + i) * halo_per_tile - 1, 0), 0)),
            pl.BlockSpec(wp.shape, lambda b, i: (0, 0, 0)),
            pl.BlockSpec((1, c), const),
            pl.BlockSpec(w_out.shape, const, pipeline_mode=pl.Buffered(1)),
        ],
        out_specs=pl.BlockSpec((tm, d), row),
        out_shape=jax.ShapeDtypeStruct((m, d), F32),
        scratch_shapes=[pltpu.VMEM((tm + POOL_HALO, c), F32), pltpu.VMEM((tm, c), BF16)],
        compiler_params=pltpu.CompilerParams(
            dimension_semantics=("parallel", "arbitrary"), vmem_limit_bytes=_vmem_limit(vmem)),
        name="prompt_pool_output_projection",
    )(x, o, u, u, wp, scale, w_out)


def kernel(x_prompt, x_sample, cache_k, cache_v, state_pool, page_table, ffn1_norm, ffn1_w_gate, ffn1_w_up, ffn1_w_down, mix_norm, w_in, lambda_q1, lambda_k1, lambda_q2, lambda_k2, subln_gain, w_pool, pool_scale, w_out, ffn2_norm, ffn2_w_gate, ffn2_w_up, ffn2_w_down, final_norm):
    batch, seq, d = x_prompt.shape
    nb, n_new, _ = x_sample.shape
    depth, n_phys, page, n_heads, vd = cache_k.shape
    assert n_heads == N_HEADS
    a = n_heads * vd
    c = state_pool.shape[-1]
    past_len = page_table.shape[1] * page
    k_pool = cache_k.reshape(depth * n_phys * page * n_heads, vd)
    v_pool = cache_v.reshape(depth * n_phys * page * n_heads, vd)

    xp = x_prompt.reshape(batch * seq, d)
    xs = x_sample.reshape(nb * n_new, d)
    fg = final_norm.reshape(1, d)
    k_p, v_p, buf_p, k_s, v_s, buf_s = [], [], [], [], [], []
    for l in range(depth):
        lam_init = _lambda_init(l)
        last = l == depth - 1
        mix_g = mix_norm[l].reshape(1, d)
        w_out_b = w_out[l].astype(BF16)
        wp_b = w_pool[l].astype(BF16)
        scale = pool_scale[l].reshape(1, c)
        lams = tuple(p[l].reshape(1, -1) for p in (lambda_q1, lambda_k1, lambda_q2, lambda_k2))
        sg = subln_gain[l].reshape(1, vd)

        xp, xs = _ffn_both(xp, xs, ffn1_norm[l].reshape(1, d), ffn1_w_gate, ffn1_w_up, ffn1_w_down,
                           fg, layer=l, apply_final_norm=False)

        (qs, ks, vs, us, _, _), w_in_b = _inproj_converting(xs, mix_g, w_in, a, layer=l)

        qb, kp, vp, up, kb, vb = _inproj(xp, mix_g, w_in_b, a, layer=0)
        op = _prompt_attention(lams, sg, qb, kb, vb, batch=batch, seq=seq, lam_init=lam_init)
        xp = _prompt_outproj(xp, op, up, wp_b, scale, w_out_b, batch=batch, seq=seq)

        os_ = _sample_attention(
            page_table, lams, sg, qs.reshape(nb, n_new, a), ks.reshape(nb, n_new, a),
            vs.reshape(nb, n_new, a), k_pool, v_pool, page=page,
            first_page=l * n_phys, lam_init=lam_init)
        us3 = us.reshape(nb, n_new, c)
        zs_t = _sample_pool(jnp.transpose(state_pool[l], (1, 0, 2)), jnp.transpose(us3, (1, 0, 2)),
                            wp_b, scale, pos0=past_len)
        zs = jnp.transpose(zs_t, (1, 0, 2)).reshape(nb * n_new, c)
        xs = _outproj(xs, os_.reshape(nb * n_new, a), zs, w_out_b)

        xp, xs = _ffn_both(xp, xs, ffn2_norm[l].reshape(1, d), ffn2_w_gate, ffn2_w_up, ffn2_w_down,
                           fg, layer=l, apply_final_norm=last)

        k_p.append(kp.reshape(batch, seq, n_heads, vd))
        v_p.append(vp.reshape(batch, seq, n_heads, vd))
        buf_p.append(up.reshape(batch, seq, c)[:, seq - POOL_BUF:])
        k_s.append(ks.reshape(nb, n_new, n_heads, vd))
        v_s.append(vs.reshape(nb, n_new, n_heads, vd))
        buf_s.append(jnp.concatenate([state_pool[l], us3], axis=1)[:, -POOL_BUF:])

    return (xp.reshape(batch, seq, d), xs.reshape(nb, n_new, d),
            jnp.stack(k_p), jnp.stack(v_p), jnp.stack(buf_p),
            jnp.stack(k_s), jnp.stack(v_s), jnp.stack(buf_s))
```

```python
import functools
import math

import jax
import jax.numpy as jnp
from jax import lax
from jax.experimental import pallas as pl
from jax.experimental.pallas import tpu as pltpu

F32 = jnp.float32
BF16 = jnp.bfloat16

N_HEADS = 8
POOL_WINDOWS = (2, 4, 8, 16)
POOL_BUF = max(POOL_WINDOWS) - 1
FFN_RESIDUAL = 0.5
NORM_EPS = 1e-6
SUBLN_EPS = 1e-5
MASK_VALUE = -1e30

V7X_VMEM_BYTES = 64 * 1024 * 1024
LANES = 128

FFN_TOKEN_TILE = 512
FFN_HIDDEN_TILE = 1024
FFN_EMIT_HIDDEN_TILE = 256
PROJ_TOKEN_TILE = 512
ATTN_Q_TILE = 256
POOL_TOKEN_TILE = 512
POOL_HALO = 16
SAMPLE_KEY_PAD = 128


def _vmem_limit(nbytes):
    return int(min(nbytes * 5 // 4 + (8 << 20), V7X_VMEM_BYTES - (4 << 20)))


def _rmsnorm(x, g, eps):
    return x * lax.rsqrt(jnp.mean(x * x, axis=-1, keepdims=True) + eps) * g


def _lambda_init(layer):
    return 0.8 - 0.6 * math.exp(-0.3 * layer)


def _diff_lambda(lq1_ref, lk1_ref, lq2_ref, lk2_ref, lam_init):
    a = jnp.sum(lq1_ref[...] * lk1_ref[...], axis=-1, keepdims=True)
    b = jnp.sum(lq2_ref[...] * lk2_ref[...], axis=-1, keepdims=True)
    return jnp.exp(a) - jnp.exp(b) + lam_init


def _swiglu_partial(h, wg, wu, wd):
    gate = jnp.dot(h, wg, preferred_element_type=F32)
    up = jnp.dot(h, wu, preferred_element_type=F32)
    act = (FFN_RESIDUAL * (gate * jax.nn.sigmoid(gate)) * up).astype(BF16)
    return jnp.dot(act, wd, preferred_element_type=F32)


def _ffn_body(x_ref, g_ref, wg_ref, wu_ref, wd_ref, wg_rem_ref, wu_rem_ref, wd_rem_ref, fg_ref,
              o_ref, *rest, apply_final_norm, emit_bf16_weights):
    f = pl.program_id(1)
    h_ref = rest[-1]

    @pl.when(f == 0)
    def _():
        x = x_ref[...]
        h = _rmsnorm(x, g_ref[...], NORM_EPS).astype(BF16)
        h_ref[...] = h
        o_ref[...] = x + _swiglu_partial(h, wg_rem_ref[...], wu_rem_ref[...], wd_rem_ref[...])

    if emit_bf16_weights:
        wg, wu, wd = (r[...].astype(BF16) for r in (wg_ref, wu_ref, wd_ref))
        for out_ref, w in zip(rest[:3], (wg, wu, wd)):
            out_ref[...] = w
    else:
        wg, wu, wd = wg_ref[...], wu_ref[...], wd_ref[...]
    o_ref[...] += _swiglu_partial(h_ref[...], wg, wu, wd)

    if apply_final_norm:
        @pl.when(f == pl.num_programs(1) - 1)
        def _():
            o_ref[...] = _rmsnorm(o_ref[...], fg_ref[...], NORM_EPS)


def _ffn(x, g, wg, wu, wd, w_rem, fg, *, layer, n_main, apply_final_norm,
         emit_bf16_weights=False):
    m, d = x.shape
    wg_rem, wu_rem, wd_rem = w_rem
    rem = wg_rem.shape[1]
    tm = min(FFN_TOKEN_TILE, m)
    tf = FFN_EMIT_HIDDEN_TILE if emit_bf16_weights else FFN_HIDDEN_TILE
    assert m % tm == 0 and n_main % tf == 0 and rem % LANES == 0
    assert wg.dtype == (F32 if emit_bf16_weights else BF16)
    assert not emit_bf16_weights or m == tm
    const = lambda i, f: (0, 0)
    resident = functools.partial(pl.BlockSpec, index_map=const, pipeline_mode=pl.Buffered(1))
    col_tile = pl.BlockSpec((None, d, tf), lambda i, f: (layer, 0, f))
    row_tile = pl.BlockSpec((None, tf, d), lambda i, f: (layer, f, 0))
    tok_tile = pl.BlockSpec((tm, d), lambda i, f: (i, 0))
    out_specs, out_shape = [tok_tile], [jax.ShapeDtypeStruct((m, d), F32)]
    if emit_bf16_weights:
        out_specs += [pl.BlockSpec((None, d, tf), lambda i, f: (0, 0, f))] * 2
        out_specs += [pl.BlockSpec((None, tf, d), lambda i, f: (0, f, 0))]
        out_shape += [jax.ShapeDtypeStruct((1, d, n_main), BF16)] * 2
        out_shape += [jax.ShapeDtypeStruct((1, n_main, d), BF16)]
    w_bytes = wg.dtype.itemsize + (2 if emit_bf16_weights else 0)
    vmem = (2 * 2 * tm * d * 4 + tm * d * 2 + 2 * 3 * d * tf * w_bytes + 3 * d * rem * 2
            + 3 * tm * tf * 4)
    outs = pl.pallas_call(
        functools.partial(_ffn_body, apply_final_norm=apply_final_norm,
                          emit_bf16_weights=emit_bf16_weights),
        grid=(m // tm, n_main // tf),
        in_specs=[
            tok_tile,
            pl.BlockSpec((1, d), const),
            col_tile, col_tile, row_tile,
            resident((d, rem)), resident((d, rem)), resident((rem, d)),
            pl.BlockSpec((1, d), const),
        ],
        out_specs=out_specs,
        out_shape=out_shape,
        scratch_shapes=[pltpu.VMEM((tm, d), BF16)],
        compiler_params=pltpu.CompilerParams(
            dimension_semantics=("parallel", "arbitrary"),
            vmem_limit_bytes=_vmem_limit(vmem)),
        name="ffn_half_step",
    )(x, g, wg, wu, wd, wg_rem, wu_rem, wd_rem, fg)
    return outs if emit_bf16_weights else outs[0]


def _ffn_both(xp, xs, g, wg, wu, wd, fg, *, layer, apply_final_norm):
    d_ff = wg.shape[-1]
    n_main = d_ff // FFN_HIDDEN_TILE * FFN_HIDDEN_TILE
    w_rem = (wg[layer, :, n_main:].astype(BF16), wu[layer, :, n_main:].astype(BF16),
             wd[layer, n_main:, :].astype(BF16))
    xs, wg_b, wu_b, wd_b = _ffn(xs, g, wg, wu, wd, w_rem, fg, layer=layer, n_main=n_main,
                                apply_final_norm=apply_final_norm, emit_bf16_weights=True)
    xp = _ffn(xp, g, wg_b, wu_b, wd_b, w_rem, fg, layer=0, n_main=n_main,
              apply_final_norm=apply_final_norm)
    return xp, xs


def _inproj_body(x_ref, g_ref, w_ref, q_ref, k_ref, v_ref, u_ref, kb_ref, vb_ref, *, q_scale):
    h = _rmsnorm(x_ref[...], g_ref[...], NORM_EPS).astype(BF16)
    a = q_ref.shape[1]
    q_ref[...] = (jnp.dot(h, w_ref[:, 0:a], preferred_element_type=F32) * q_scale).astype(BF16)
    k = jnp.dot(h, w_ref[:, a:2 * a], preferred_element_type=F32)
    k_ref[...] = k
    kb_ref[...] = k.astype(BF16)
    v = jnp.dot(h, w_ref[:, 2 * a:3 * a], preferred_element_type=F32)
    v_ref[...] = v
    vb_ref[...] = v.astype(BF16)
    u_ref[...] = jnp.dot(h, w_ref[:, 3 * a:], preferred_element_type=F32)


def _inproj_outputs(m, a, pw):
    return [jax.ShapeDtypeStruct((m, a), BF16), jax.ShapeDtypeStruct((m, a), F32),
            jax.ShapeDtypeStruct((m, a), F32), jax.ShapeDtypeStruct((m, pw), F32),
            jax.ShapeDtypeStruct((m, a), BF16), jax.ShapeDtypeStruct((m, a), BF16)]


def _inproj_converting_body(x_ref, g_ref, w_ref, q_ref, k_ref, v_ref, u_ref, kb_ref, vb_ref,
                            wb_ref, h_ref, *, q_scale):
    j = pl.program_id(0)

    @pl.when(j == 0)
    def _():
        h_ref[...] = _rmsnorm(x_ref[...], g_ref[...], NORM_EPS).astype(BF16)

    w = w_ref[...].astype(BF16)
    wb_ref[...] = w
    y = jnp.dot(h_ref[...], w, preferred_element_type=F32)

    @pl.when(j == 0)
    def _():
        q_ref[...] = (y * q_scale).astype(BF16)

    @pl.when(j == 1)
    def _():
        k_ref[...] = y
        kb_ref[...] = y.astype(BF16)

    @pl.when(j == 2)
    def _():
        v_ref[...] = y
        vb_ref[...] = y.astype(BF16)

    @pl.when(j == 3)
    def _():
        u_ref[...] = y


def _inproj_converting(x, g, w_in, attn_width, *, layer):
    m, d = x.shape
    n = w_in.shape[-1]
    a = attn_width
    head_dim = a // (2 * N_HEADS)
    assert n == 4 * a and m <= PROJ_TOKEN_TILE
    whole = lambda j: (0, 0)
    vmem = (m * d * 4 + 2 * d * a * (4 + 2) + 2 * m * a * (2 + 4 + 4 + 4 + 2 + 2) + m * d * 2
            + m * a * 4)
    *outs, w_bf16 = pl.pallas_call(
        functools.partial(_inproj_converting_body, q_scale=head_dim ** -0.5),
        grid=(4,),
        in_specs=[
            pl.BlockSpec((m, d), whole, pipeline_mode=pl.Buffered(1)),
            pl.BlockSpec((1, d), whole),
            pl.BlockSpec((None, d, a), lambda j: (layer, 0, j)),
        ],
        out_specs=[pl.BlockSpec((m, a), whole)] * 6
                  + [pl.BlockSpec((None, d, a), lambda j: (0, 0, j))],
        out_shape=_inproj_outputs(m, a, a) + [jax.ShapeDtypeStruct((1, d, n), BF16)],
        scratch_shapes=[pltpu.VMEM((m, d), BF16)],
        compiler_params=pltpu.CompilerParams(
            dimension_semantics=("arbitrary",), vmem_limit_bytes=_vmem_limit(vmem)),
        name="input_projection_converting",
    )(x, g, w_in)
    return outs, w_bf16


def _inproj(x, g, w_in, attn_width, *, layer):
    m, d = x.shape
    n = w_in.shape[-1]
    a = attn_width
    pw = n - 3 * a
    head_dim = a // (2 * N_HEADS)
    tm = min(PROJ_TOKEN_TILE, m)
    assert m % tm == 0
    vmem = 2 * tm * d * 4 + d * n * 2 + 2 * tm * (a * 2 + 2 * a * 6 + pw * 4) + tm * n * 4
    row = lambda i: (i, 0)
    return pl.pallas_call(
        functools.partial(_inproj_body, q_scale=head_dim ** -0.5),
        grid=(m // tm,),
        in_specs=[
            pl.BlockSpec((tm, d), row),
            pl.BlockSpec((1, d), lambda i: (0, 0)),
            pl.BlockSpec((None, d, n), lambda i: (layer, 0, 0), pipeline_mode=pl.Buffered(1)),
        ],
        out_specs=[
            pl.BlockSpec((tm, a), row), pl.BlockSpec((tm, a), row), pl.BlockSpec((tm, a), row),
            pl.BlockSpec((tm, pw), row), pl.BlockSpec((tm, a), row), pl.BlockSpec((tm, a), row),
        ],
        out_shape=_inproj_outputs(m, a, pw),
        compiler_params=pltpu.CompilerParams(
            dimension_semantics=("parallel",), vmem_limit_bytes=_vmem_limit(vmem)),
        name="input_projection",
    )(x, g, w_in)


def _subln(o, gain, lam_init):
    return _rmsnorm(o, gain, SUBLN_EPS) * (1.0 - lam_init)


def _prompt_tiles(tiles, lam, sg_ref, q_ref, k_ref, v_ref, o_ref, *, lam_init):
    seq, vd = q_ref.shape
    hd = vd // 2
    tq = ATTN_Q_TILE
    nt = (((1,), (1,)), ((), ()))
    lane = lax.broadcasted_iota(jnp.int32, (tq, vd), 1)
    row = lax.broadcasted_iota(jnp.int32, (2 * tq, tq), 0)
    row = jnp.where(row >= tq, row - tq, row)
    col = lax.broadcasted_iota(jnp.int32, (2 * tq, tq), 1)
    causal = col <= row

    for i in tiles:
        lo, hi = i * tq, (i + 1) * tq
        q = q_ref[lo:hi, :]
        zero = jnp.zeros_like(q)
        qz = jnp.concatenate([jnp.where(lane < hd, q, zero), jnp.where(lane >= hd, q, zero)],
                             axis=0)
        s_diag = lax.dot_general(qz, k_ref[lo:hi, :], nt, preferred_element_type=F32)
        s_diag = jnp.where(causal, s_diag, MASK_VALUE)
        m = jnp.max(s_diag, axis=-1, keepdims=True)
        if i > 0:
            s_past = lax.dot_general(qz, k_ref[0:lo, :], nt, preferred_element_type=F32)
            m = jnp.maximum(m, jnp.max(s_past, axis=-1, keepdims=True))
        p_diag = jnp.exp(s_diag - m)
        l = jnp.sum(p_diag, axis=-1, keepdims=True)
        acc = jnp.dot(p_diag.astype(BF16), v_ref[lo:hi, :], preferred_element_type=F32)
        if i > 0:
            p_past = jnp.exp(s_past - m)
            l = l + jnp.sum(p_past, axis=-1, keepdims=True)
            acc = acc + jnp.dot(p_past.astype(BF16), v_ref[0:lo, :], preferred_element_type=F32)
        on = acc * (1.0 / l)
        o = on[:tq] - lam * on[tq:]
        o_ref[lo:hi, :] = _subln(o, sg_ref[...], lam_init).astype(o_ref.dtype)


def _sample_sequence(lam, sg_ref, q_ref, kn_ref, vn_ref, k_pages, v_pages, o_ref, kbf_ref, vbf_ref,
                     *, lam_init):
    n_pages = len(k_pages)
    page_rows = k_pages[0].shape[0]
    past_rows = n_pages * page_rows
    n_rows_k = kbf_ref.shape[0]
    n_new, a = q_ref.shape
    vd = a // N_HEADS
    hd = vd // 2
    head_cols = [slice(h * vd, (h + 1) * vd) for h in range(N_HEADS)]

    for j in range(n_pages):
        rows = slice(j * page_rows, (j + 1) * page_rows)
        kbf_ref[rows, :] = k_pages[j][...].astype(BF16)
        vbf_ref[rows, :] = v_pages[j][...].astype(BF16)
    kn, vn = kn_ref[...], vn_ref[...]
    pad = jnp.zeros((n_rows_k - past_rows - N_HEADS * n_new, vd), F32)
    kbf_ref[past_rows:, :] = jnp.concatenate([kn[:, hc] for hc in head_cols] + [pad],
                                             axis=0).astype(BF16)
    vbf_ref[past_rows:, :] = jnp.concatenate([vn[:, hc] for hc in head_cols] + [pad],
                                             axis=0).astype(BF16)

    qf = q_ref[...].astype(F32)
    n_q = 2 * N_HEADS * n_new
    r = lax.broadcasted_iota(jnp.int32, (n_q, vd), 0)
    c = lax.broadcasted_iota(jnp.int32, (n_q, vd), 1)
    q2 = jnp.concatenate([qf[:, hc] for hc in head_cols] * 2, axis=0)
    q2 = jnp.where((c // hd) == (r // (N_HEADS * n_new)), q2, 0.0).astype(BF16)

    col = lax.broadcasted_iota(jnp.int32, (n_q, n_rows_k), 1)
    row = lax.broadcasted_iota(jnp.int32, (n_q, n_rows_k), 0)
    q_head, q_tok = (row // n_new) % N_HEADS, row % n_new
    new_col = col - past_rows
    visible = (((col < past_rows) & ((col % N_HEADS) == q_head))
               | ((new_col >= 0) & (new_col < N_HEADS * n_new) & ((new_col // n_new) == q_head)
                  & ((new_col % n_new) <= q_tok)))

    s = lax.dot_general(q2, kbf_ref[...], (((1,), (1,)), ((), ())),
                        preferred_element_type=F32)
    s = jnp.where(visible, s, MASK_VALUE)
    p = jnp.exp(s - jnp.max(s, axis=-1, keepdims=True))
    p = p * (1.0 / jnp.sum(p, axis=-1, keepdims=True))
    attn = (p[:n_q // 2] - lam * p[n_q // 2:]).astype(BF16)
    o = jnp.dot(attn, vbf_ref[...], preferred_element_type=F32)
    o = _subln(o, sg_ref[...], lam_init)
    for h, hc in enumerate(head_cols):
        o_ref[:, hc] = o[h * n_new:(h + 1) * n_new]


def _attention_body(pt_ref, lq1_ref, lk1_ref, lq2_ref, lk2_ref, sg_ref, qs_ref, kn_ref, vn_ref,
                    qp_ref, kp_ref, vp_ref, *rest, n_pages, n_q_tiles, lam_init):
    del pt_ref
    k_pages = rest[:n_pages]
    v_pages = rest[n_pages:2 * n_pages]
    os_ref, op_ref, kbf_ref, vbf_ref = rest[2 * n_pages:]
    lam = _diff_lambda(lq1_ref, lk1_ref, lq2_ref, lk2_ref, lam_init)
    _sample_sequence(lam, sg_ref, qs_ref, kn_ref, vn_ref, k_pages, v_pages, os_ref, kbf_ref,
                     vbf_ref, lam_init=lam_init)
    n_pairs = n_q_tiles // 2
    pair = pl.program_id(0) % n_pairs
    for i in range(n_pairs):
        @pl.when(pair == i)
        def _(i=i):
            _prompt_tiles((i, n_q_tiles - 1 - i), lam, sg_ref, qp_ref, kp_ref, vp_ref, op_ref,
                          lam_init=lam_init)


def _attention(page_table, lams, sg, qs, k_new, v_new, k_pool, v_pool, qb, kb, vb, *, page,
               first_page, batch, seq, lam_init):
    nb, n_new, a = qs.shape
    n_pages = page_table.shape[1]
    vd = a // N_HEADS
    page_rows = page * N_HEADS
    n_keys = n_pages * page + SAMPLE_KEY_PAD
    n_q_tiles = seq // ATTN_Q_TILE
    n_pairs = n_q_tiles // 2
    assert seq % (2 * ATTN_Q_TILE) == 0 and batch * N_HEADS * n_pairs == nb
    const = lambda s, pt: (0, 0)
    tok_spec = pl.BlockSpec((None, n_new, a), lambda s, pt: (s, 0, 0))
    head_spec = pl.BlockSpec((seq, vd),
                             lambda s, pt: (s // n_pairs // N_HEADS, s // n_pairs % N_HEADS))
    page_specs = [pl.BlockSpec((page_rows, vd), lambda s, pt, j=j: (first_page + pt[s, j], 0))
                  for j in range(n_pages)]
    vmem = (2 * 2 * n_pages * page * a * 4 + 2 * 4 * seq * vd * 2
            + 5 * 2 * ATTN_Q_TILE * seq * 4)
    return pl.pallas_call(
        functools.partial(_attention_body, n_pages=n_pages, n_q_tiles=n_q_tiles,
                          lam_init=lam_init),
        grid_spec=pltpu.PrefetchScalarGridSpec(
            num_scalar_prefetch=1,
            grid=(nb,),
            in_specs=[pl.BlockSpec((1, vd // 2), const)] * 4 + [pl.BlockSpec((1, vd), const)]
                     + [tok_spec] * 3 + [head_spec] * 3 + page_specs + page_specs,
            out_specs=[tok_spec, head_spec],
            scratch_shapes=[pltpu.VMEM((n_keys * N_HEADS, vd), BF16)] * 2,
        ),
        out_shape=[jax.ShapeDtypeStruct((nb, n_new, a), F32),
                   jax.ShapeDtypeStruct(qb.shape, BF16)],
        compiler_params=pltpu.CompilerParams(
            dimension_semantics=("arbitrary",), vmem_limit_bytes=_vmem_limit(vmem)),
        name="diff_attention",
    )(page_table, *lams, sg, qs, k_new, v_new, qb, kb, vb,
      *([k_pool] * n_pages), *([v_pool] * n_pages))


def _pool_project(d, wp_ref, scale_ref, g, cols):
    return jnp.dot(d.astype(BF16), wp_ref[g], preferred_element_type=F32) * scale_ref[:, cols]


def _prompt_pool_tile(i, u_ref, halo_ref, wp_ref, scale_ref, z_ref, ext_ref):
    tm, c = u_ref.shape
    cg = c // len(POOL_WINDOWS)
    ext_ref[0:POOL_HALO, :] = jnp.where(i == 0, 0.0, halo_ref[...])
    ext_ref[POOL_HALO:, :] = u_ref[...]
    pos = i * tm + lax.broadcasted_iota(jnp.int32, (tm, 1), 0)
    for g, w in enumerate(POOL_WINDOWS):
        cols = slice(g * cg, (g + 1) * cg)
        cur = ext_ref[POOL_HALO:, cols]
        win = cur
        for s in range(1, w):
            win = win + ext_ref[POOL_HALO - s:POOL_HALO - s + tm, cols]
        cnt = jnp.minimum(w, pos + 1).astype(F32)
        d = win * (1.0 / cnt) - cur
        z_ref[:, cols] = _pool_project(d, wp_ref, scale_ref, g, cols).astype(z_ref.dtype)


def _sample_pool_body(state_ref, u_ref, wp_ref, scale_ref, z_ref, *, pos0):
    n_state = state_ref.shape[0]
    n_new, _, c = u_ref.shape
    cg = c // len(POOL_WINDOWS)

    def ext(r, cols):
        return state_ref[r, :, cols] if r < n_state else u_ref[r - n_state, :, cols]

    for g, w in enumerate(POOL_WINDOWS):
        cols = slice(g * cg, (g + 1) * cg)
        for t in range(n_new):
            cur = ext(n_state + t, cols)
            win = cur
            for s in range(1, w):
                win = win + ext(n_state + t - s, cols)
            d = win * (1.0 / min(w, pos0 + t + 1)) - cur
            z_ref[t, :, cols] = _pool_project(d, wp_ref, scale_ref, g, cols).astype(z_ref.dtype)


def _sample_pool(state_t, u_t, wp, scale, *, pos0):
    n_new, nb, c = u_t.shape
    vmem = 2 * (state_t.size + u_t.size) * 4 + wp.size * 4 + 4 * nb * c * 4
    return pl.pallas_call(
        functools.partial(_sample_pool_body, pos0=pos0),
        out_shape=jax.ShapeDtypeStruct((n_new, nb, c), BF16),
        compiler_params=pltpu.CompilerParams(vmem_limit_bytes=_vmem_limit(vmem)),
        name="sample_pool_mixer",
    )(state_t, u_t, wp, scale)


def _outproj_body(x_ref, o_ref, z_ref, w_ref, y_ref):
    a = o_ref.shape[1]
    y = x_ref[...] + jnp.dot(o_ref[...].astype(BF16), w_ref[0:a, :], preferred_element_type=F32)
    y_ref[...] = y + jnp.dot(z_ref[...].astype(BF16), w_ref[a:, :], preferred_element_type=F32)


def _outproj(x, o, z, w_out):
    m, d = x.shape
    a, c = o.shape[1], z.shape[1]
    tm = min(PROJ_TOKEN_TILE, m)
    assert m % tm == 0
    row = lambda i: (i, 0)
    vmem = 2 * 2 * tm * d * 4 + 2 * tm * (a + c) * 4 + w_out.size * 2 + tm * d * 4
    return pl.pallas_call(
        _outproj_body,
        grid=(m // tm,),
        in_specs=[
            pl.BlockSpec((tm, d), row), pl.BlockSpec((tm, a), row), pl.BlockSpec((tm, c), row),
            pl.BlockSpec(w_out.shape, lambda i: (0, 0), pipeline_mode=pl.Buffered(1)),
        ],
        out_specs=pl.BlockSpec((tm, d), row),
        out_shape=jax.ShapeDtypeStruct((m, d), F32),
        compiler_params=pltpu.CompilerParams(
            dimension_semantics=("parallel",), vmem_limit_bytes=_vmem_limit(vmem)),
        name="output_projection",
    )(x, o, z, w_out)


def _prompt_outproj_body(x_ref, o_ref, u_ref, halo_ref, wp_ref, scale_ref, w_ref, y_ref,
                         ext_ref, z_ref):
    a = o_ref.shape[1]
    y = x_ref[...] + jnp.dot(o_ref[...], w_ref[0:a, :], preferred_element_type=F32)
    _prompt_pool_tile(pl.program_id(1), u_ref, halo_ref, wp_ref, scale_ref, z_ref, ext_ref)
    y_ref[...] = y + jnp.dot(z_ref[...], w_ref[a:, :], preferred_element_type=F32)


def _prompt_outproj(x, o, u, wp, scale, w_out, *, batch, seq):
    m, d = x.shape
    a, c = o.shape[1], u.shape[1]
    tm = POOL_TOKEN_TILE
    nt = seq // tm
    halo_per_tile = tm // POOL_HALO
    assert seq % tm == 0 and o.dtype == BF16
    row = lambda b, i: (b * nt + i, 0)
    const = lambda b, i: (0, 0)
    vmem = (2 * 2 * tm * d * 4 + 2 * tm * (a * 2 + c * 4) + w_out.size * 2 + wp.size * 2 * 2
            + (tm + POOL_HALO) * c * 4 + tm * c * 2 + tm * d * 4 + 4 * tm * c * 4)
    return pl.pallas_call(
        _prompt_outproj_body,
        grid=(batch, nt),
        in_specs=[
            pl.BlockSpec((tm, d), row), pl.BlockSpec((tm, a), row), pl.BlockSpec((tm, c), row),
            pl.BlockSpec((POOL_HALO, c),
                         lambda b, i: (jnp.maximum((b * nt + i) * halo_per_tile - 1, 0), 0)),
            pl.BlockSpec(wp.shape, lambda b, i: (0, 0, 0)),
            pl.BlockSpec((1, c), const),
            pl.BlockSpec(w_out.shape, const, pipeline_mode=pl.Buffered(1)),
        ],
        out_specs=pl.BlockSpec((tm, d), row),
        out_shape=jax.ShapeDtypeStruct((m, d), F32),
        scratch_shapes=[pltpu.VMEM((tm + POOL_HALO, c), F32), pltpu.VMEM((tm, c), BF16)],
        compiler_params=pltpu.CompilerParams(
            dimension_semantics=("parallel", "arbitrary"), vmem_limit_bytes=_vmem_limit(vmem)),
        name="prompt_pool_output_projection",
    )(x, o, u, u, wp, scale, w_out)


def kernel(x_prompt, x_sample, cache_k, cache_v, state_pool, page_table, ffn1_norm, ffn1_w_gate, ffn1_w_up, ffn1_w_down, mix_norm, w_in, lambda_q1, lambda_k1, lambda_q2, lambda_k2, subln_gain, w_pool, pool_scale, w_out, ffn2_norm, ffn2_w_gate, ffn2_w_up, ffn2_w_down, final_norm):
    batch, seq, d = x_prompt.shape
    nb, n_new, _ = x_sample.shape
    depth, n_phys, page, n_heads, vd = cache_k.shape
    assert n_heads == N_HEADS
    a = n_heads * vd
    c = state_pool.shape[-1]
    past_len = page_table.shape[1] * page
    k_pool = cache_k.reshape(depth * n_phys * page * n_heads, vd)
    v_pool = cache_v.reshape(depth * n_phys * page * n_heads, vd)

    xp = x_prompt.reshape(batch * seq, d)
    xs = x_sample.reshape(nb * n_new, d)
    fg = final_norm.reshape(1, d)
    k_p, v_p, buf_p, k_s, v_s, buf_s = [], [], [], [], [], []
    for l in range(depth):
        lam_init = _lambda_init(l)
        last = l == depth - 1
        mix_g = mix_norm[l].reshape(1, d)
        w_out_b = w_out[l].astype(BF16)
        wp_b = w_pool[l].astype(BF16)
        scale = pool_scale[l].reshape(1, c)
        lams = tuple(p[l].reshape(1, -1) for p in (lambda_q1, lambda_k1, lambda_q2, lambda_k2))
        sg = subln_gain[l].reshape(1, vd)

        xp, xs = _ffn_both(xp, xs, ffn1_norm[l].reshape(1, d), ffn1_w_gate, ffn1_w_up, ffn1_w_down,
                           fg, layer=l, apply_final_norm=False)

        (qs, ks, vs, us, _, _), w_in_b = _inproj_converting(xs, mix_g, w_in, a, layer=l)

        qb, kp, vp, up, kb, vb = _inproj(xp, mix_g, w_in_b, a, layer=0)
        os_, op = _attention(
            page_table, lams, sg, qs.reshape(nb, n_new, a), ks.reshape(nb, n_new, a),
            vs.reshape(nb, n_new, a), k_pool, v_pool, qb, kb, vb, page=page,
            first_page=l * n_phys, batch=batch, seq=seq, lam_init=lam_init)
        xp = _prompt_outproj(xp, op, up, wp_b, scale, w_out_b, batch=batch, seq=seq)

        us3 = us.reshape(nb, n_new, c)
        zs_t = _sample_pool(jnp.transpose(state_pool[l], (1, 0, 2)), jnp.transpose(us3, (1, 0, 2)),
                            wp_b, scale, pos0=past_len)
        zs = jnp.transpose(zs_t, (1, 0, 2)).reshape(nb * n_new, c)
        xs = _outproj(xs, os_.reshape(nb * n_new, a), zs, w_out_b)

        xp, xs = _ffn_both(xp, xs, ffn2_norm[l].reshape(1, d), ffn2_w_gate, ffn2_w_up, ffn2_w_down,
                           fg, layer=l, apply_final_norm=last)

        k_p.append(kp.reshape(batch, seq, n_heads, vd))
        v_p.append(vp.reshape(batch, seq, n_heads, vd))
        buf_p.append(up.reshape(batch, seq, c)[:, seq - POOL_BUF:])
        k_s.append(ks.reshape(nb, n_new, n_heads, vd))
        v_s.append(vs.reshape(nb, n_new, n_heads, vd))
        buf_s.append(jnp.concatenate([state_pool[l], us3], axis=1)[:, -POOL_BUF:])

    return (xp.reshape(batch, seq, d), xs.reshape(nb, n_new, d),
            jnp.stack(k_p), jnp.stack(v_p), jnp.stack(buf_p),
            jnp.stack(k_s), jnp.stack(v_s), jnp.stack(buf_s))
```

```python
import functools
import math

import jax
import jax.numpy as jnp
from jax import lax
from jax.experimental import pallas as pl
from jax.experimental.pallas import tpu as pltpu

F32 = jnp.float32
BF16 = jnp.bfloat16

N_HEADS = 8
POOL_WINDOWS = (2, 4, 8, 16)
POOL_BUF = max(POOL_WINDOWS) - 1
FFN_RESIDUAL = 0.5
NORM_EPS = 1e-6
SUBLN_EPS = 1e-5
MASK_VALUE = -1e30

V7X_VMEM_BYTES = 64 * 1024 * 1024
LANES = 128

FFN_TOKEN_TILE = 512
FFN_HIDDEN_TILE = 1024
FFN_EMIT_HIDDEN_TILE = 256
PROJ_TOKEN_TILE = 512
ATTN_Q_TILE = 256
POOL_TOKEN_TILE = 512
POOL_HALO = 16
SAMPLE_HEAD_GROUPS = 2
SAMPLE_KEY_PAD = 128


def _vmem_limit(nbytes):
    return int(min(nbytes * 5 // 4 + (8 << 20), V7X_VMEM_BYTES - (4 << 20)))


def _rmsnorm(x, g, eps):
    return x * lax.rsqrt(jnp.mean(x * x, axis=-1, keepdims=True) + eps) * g


def _lambda_init(layer):
    return 0.8 - 0.6 * math.exp(-0.3 * layer)


def _diff_lambda(lq1_ref, lk1_ref, lq2_ref, lk2_ref, lam_init):
    a = jnp.sum(lq1_ref[...] * lk1_ref[...], axis=-1, keepdims=True)
    b = jnp.sum(lq2_ref[...] * lk2_ref[...], axis=-1, keepdims=True)
    return jnp.exp(a) - jnp.exp(b) + lam_init


def _swiglu_partial(h, wg, wu, wd):
    gate = jnp.dot(h, wg, preferred_element_type=F32)
    up = jnp.dot(h, wu, preferred_element_type=F32)
    act = (FFN_RESIDUAL * (gate * jax.nn.sigmoid(gate)) * up).astype(BF16)
    return jnp.dot(act, wd, preferred_element_type=F32)


def _ffn_body(x_ref, g_ref, wg_ref, wu_ref, wd_ref, wg_rem_ref, wu_rem_ref, wd_rem_ref, fg_ref,
              o_ref, *rest, apply_final_norm, emit_bf16_weights):
    f = pl.program_id(1)
    h_ref = rest[-1]

    @pl.when(f == 0)
    def _():
        x = x_ref[...]
        h = _rmsnorm(x, g_ref[...], NORM_EPS).astype(BF16)
        h_ref[...] = h
        o_ref[...] = x + _swiglu_partial(h, wg_rem_ref[...], wu_rem_ref[...], wd_rem_ref[...])

    if emit_bf16_weights:
        wg, wu, wd = (r[...].astype(BF16) for r in (wg_ref, wu_ref, wd_ref))
        for out_ref, w in zip(rest[:3], (wg, wu, wd)):
            out_ref[...] = w
    else:
        wg, wu, wd = wg_ref[...], wu_ref[...], wd_ref[...]
    o_ref[...] += _swiglu_partial(h_ref[...], wg, wu, wd)

    if apply_final_norm:
        @pl.when(f == pl.num_programs(1) - 1)
        def _():
            o_ref[...] = _rmsnorm(o_ref[...], fg_ref[...], NORM_EPS)


def _ffn(x, g, wg, wu, wd, w_rem, fg, *, layer, n_main, apply_final_norm,
         emit_bf16_weights=False):
    m, d = x.shape
    wg_rem, wu_rem, wd_rem = w_rem
    rem = wg_rem.shape[1]
    tm = min(FFN_TOKEN_TILE, m)
    tf = FFN_EMIT_HIDDEN_TILE if emit_bf16_weights else FFN_HIDDEN_TILE
    assert m % tm == 0 and n_main % tf == 0 and rem % LANES == 0
    assert wg.dtype == (F32 if emit_bf16_weights else BF16)
    assert not emit_bf16_weights or m == tm
    const = lambda i, f: (0, 0)
    resident = functools.partial(pl.BlockSpec, index_map=const, pipeline_mode=pl.Buffered(1))
    col_tile = pl.BlockSpec((None, d, tf), lambda i, f: (layer, 0, f))
    row_tile = pl.BlockSpec((None, tf, d), lambda i, f: (layer, f, 0))
    tok_tile = pl.BlockSpec((tm, d), lambda i, f: (i, 0))
    out_specs, out_shape = [tok_tile], [jax.ShapeDtypeStruct((m, d), F32)]
    if emit_bf16_weights:
        out_specs += [pl.BlockSpec((None, d, tf), lambda i, f: (0, 0, f))] * 2
        out_specs += [pl.BlockSpec((None, tf, d), lambda i, f: (0, f, 0))]
        out_shape += [jax.ShapeDtypeStruct((1, d, n_main), BF16)] * 2
        out_shape += [jax.ShapeDtypeStruct((1, n_main, d), BF16)]
    w_bytes = wg.dtype.itemsize + (2 if emit_bf16_weights else 0)
    vmem = (2 * 2 * tm * d * 4 + tm * d * 2 + 2 * 3 * d * tf * w_bytes + 3 * d * rem * 2
            + 3 * tm * tf * 4)
    outs = pl.pallas_call(
        functools.partial(_ffn_body, apply_final_norm=apply_final_norm,
                          emit_bf16_weights=emit_bf16_weights),
        grid=(m // tm, n_main // tf),
        in_specs=[
            tok_tile,
            pl.BlockSpec((1, d), const),
            col_tile, col_tile, row_tile,
            resident((d, rem)), resident((d, rem)), resident((rem, d)),
            pl.BlockSpec((1, d), const),
        ],
        out_specs=out_specs,
        out_shape=out_shape,
        scratch_shapes=[pltpu.VMEM((tm, d), BF16)],
        compiler_params=pltpu.CompilerParams(
            dimension_semantics=("parallel", "arbitrary"),
            vmem_limit_bytes=_vmem_limit(vmem)),
        name="ffn_half_step",
    )(x, g, wg, wu, wd, wg_rem, wu_rem, wd_rem, fg)
    return outs if emit_bf16_weights else outs[0]


def _ffn_both(xp, xs, g, wg, wu, wd, fg, *, layer, apply_final_norm):
    d_ff = wg.shape[-1]
    n_main = d_ff // FFN_HIDDEN_TILE * FFN_HIDDEN_TILE
    w_rem = (wg[layer, :, n_main:].astype(BF16), wu[layer, :, n_main:].astype(BF16),
             wd[layer, n_main:, :].astype(BF16))
    xs, wg_b, wu_b, wd_b = _ffn(xs, g, wg, wu, wd, w_rem, fg, layer=layer, n_main=n_main,
                                apply_final_norm=apply_final_norm, emit_bf16_weights=True)
    xp = _ffn(xp, g, wg_b, wu_b, wd_b, w_rem, fg, layer=0, n_main=n_main,
              apply_final_norm=apply_final_norm)
    return xp, xs


def _inproj_body(x_ref, g_ref, w_ref, q_ref, k_ref, v_ref, u_ref, kb_ref, vb_ref, *, q_scale):
    h = _rmsnorm(x_ref[...], g_ref[...], NORM_EPS).astype(BF16)
    a = q_ref.shape[1]
    q_ref[...] = (jnp.dot(h, w_ref[:, 0:a], preferred_element_type=F32) * q_scale).astype(BF16)
    k = jnp.dot(h, w_ref[:, a:2 * a], preferred_element_type=F32)
    k_ref[...] = k
    kb_ref[...] = k.astype(BF16)
    v = jnp.dot(h, w_ref[:, 2 * a:3 * a], preferred_element_type=F32)
    v_ref[...] = v
    vb_ref[...] = v.astype(BF16)
    u_ref[...] = jnp.dot(h, w_ref[:, 3 * a:], preferred_element_type=F32)


def _inproj_outputs(m, a, pw):
    return [jax.ShapeDtypeStruct((m, a), BF16), jax.ShapeDtypeStruct((m, a), F32),
            jax.ShapeDtypeStruct((m, a), F32), jax.ShapeDtypeStruct((m, pw), F32),
            jax.ShapeDtypeStruct((m, a), BF16), jax.ShapeDtypeStruct((m, a), BF16)]


def _inproj_converting_body(x_ref, g_ref, w_ref, q_ref, k_ref, v_ref, u_ref, kb_ref, vb_ref,
                            wb_ref, h_ref, *, q_scale):
    j = pl.program_id(0)

    @pl.when(j == 0)
    def _():
        h_ref[...] = _rmsnorm(x_ref[...], g_ref[...], NORM_EPS).astype(BF16)

    w = w_ref[...].astype(BF16)
    wb_ref[...] = w
    y = jnp.dot(h_ref[...], w, preferred_element_type=F32)

    @pl.when(j == 0)
    def _():
        q_ref[...] = (y * q_scale).astype(BF16)

    @pl.when(j == 1)
    def _():
        k_ref[...] = y
        kb_ref[...] = y.astype(BF16)

    @pl.when(j == 2)
    def _():
        v_ref[...] = y
        vb_ref[...] = y.astype(BF16)

    @pl.when(j == 3)
    def _():
        u_ref[...] = y


def _inproj_converting(x, g, w_in, attn_width, *, layer):
    m, d = x.shape
    n = w_in.shape[-1]
    a = attn_width
    head_dim = a // (2 * N_HEADS)
    assert n == 4 * a and m <= PROJ_TOKEN_TILE
    whole = lambda j: (0, 0)
    vmem = (m * d * 4 + 2 * d * a * (4 + 2) + 2 * m * a * (2 + 4 + 4 + 4 + 2 + 2) + m * d * 2
            + m * a * 4)
    *outs, w_bf16 = pl.pallas_call(
        functools.partial(_inproj_converting_body, q_scale=head_dim ** -0.5),
        grid=(4,),
        in_specs=[
            pl.BlockSpec((m, d), whole, pipeline_mode=pl.Buffered(1)),
            pl.BlockSpec((1, d), whole),
            pl.BlockSpec((None, d, a), lambda j: (layer, 0, j)),
        ],
        out_specs=[pl.BlockSpec((m, a), whole)] * 6
                  + [pl.BlockSpec((None, d, a), lambda j: (0, 0, j))],
        out_shape=_inproj_outputs(m, a, a) + [jax.ShapeDtypeStruct((1, d, n), BF16)],
        scratch_shapes=[pltpu.VMEM((m, d), BF16)],
        compiler_params=pltpu.CompilerParams(
            dimension_semantics=("arbitrary",), vmem_limit_bytes=_vmem_limit(vmem)),
        name="input_projection_converting",
    )(x, g, w_in)
    return outs, w_bf16


def _inproj(x, g, w_in, attn_width, *, layer):
    m, d = x.shape
    n = w_in.shape[-1]
    a = attn_width
    pw = n - 3 * a
    head_dim = a // (2 * N_HEADS)
    tm = min(PROJ_TOKEN_TILE, m)
    assert m % tm == 0
    vmem = 2 * tm * d * 4 + d * n * 2 + 2 * tm * (a * 2 + 2 * a * 6 + pw * 4) + tm * n * 4
    row = lambda i: (i, 0)
    return pl.pallas_call(
        functools.partial(_inproj_body, q_scale=head_dim ** -0.5),
        grid=(m // tm,),
        in_specs=[
            pl.BlockSpec((tm, d), row),
            pl.BlockSpec((1, d), lambda i: (0, 0)),
            pl.BlockSpec((None, d, n), lambda i: (layer, 0, 0), pipeline_mode=pl.Buffered(1)),
        ],
        out_specs=[
            pl.BlockSpec((tm, a), row), pl.BlockSpec((tm, a), row), pl.BlockSpec((tm, a), row),
            pl.BlockSpec((tm, pw), row), pl.BlockSpec((tm, a), row), pl.BlockSpec((tm, a), row),
        ],
        out_shape=_inproj_outputs(m, a, pw),
        compiler_params=pltpu.CompilerParams(
            dimension_semantics=("parallel",), vmem_limit_bytes=_vmem_limit(vmem)),
        name="input_projection",
    )(x, g, w_in)


def _subln(o, gain, lam_init):
    return _rmsnorm(o, gain, SUBLN_EPS) * (1.0 - lam_init)


def _prompt_attn_body(lq1_ref, lk1_ref, lq2_ref, lk2_ref, sg_ref, q_ref, k_ref, v_ref, o_ref, *,
                      lam_init):
    seq, vd = q_ref.shape
    hd = vd // 2
    tq = ATTN_Q_TILE
    nt = (((1,), (1,)), ((), ()))
    lam = _diff_lambda(lq1_ref, lk1_ref, lq2_ref, lk2_ref, lam_init)
    lane = lax.broadcasted_iota(jnp.int32, (tq, vd), 1)
    row = lax.broadcasted_iota(jnp.int32, (2 * tq, tq), 0)
    row = jnp.where(row >= tq, row - tq, row)
    col = lax.broadcasted_iota(jnp.int32, (2 * tq, tq), 1)
    causal = col <= row

    for i in range(seq // tq):
        lo, hi = i * tq, (i + 1) * tq
        q = q_ref[lo:hi, :]
        zero = jnp.zeros_like(q)
        qz = jnp.concatenate([jnp.where(lane < hd, q, zero), jnp.where(lane >= hd, q, zero)],
                             axis=0)
        s_diag = lax.dot_general(qz, k_ref[lo:hi, :], nt, preferred_element_type=F32)
        s_diag = jnp.where(causal, s_diag, MASK_VALUE)
        m = jnp.max(s_diag, axis=-1, keepdims=True)
        if i > 0:
            s_past = lax.dot_general(qz, k_ref[0:lo, :], nt, preferred_element_type=F32)
            m = jnp.maximum(m, jnp.max(s_past, axis=-1, keepdims=True))
        p_diag = jnp.exp(s_diag - m)
        l = jnp.sum(p_diag, axis=-1, keepdims=True)
        acc = jnp.dot(p_diag.astype(BF16), v_ref[lo:hi, :], preferred_element_type=F32)
        if i > 0:
            p_past = jnp.exp(s_past - m)
            l = l + jnp.sum(p_past, axis=-1, keepdims=True)
            acc = acc + jnp.dot(p_past.astype(BF16), v_ref[0:lo, :], preferred_element_type=F32)
        on = acc * (1.0 / l)
        o = on[:tq] - lam * on[tq:]
        o_ref[lo:hi, :] = _subln(o, sg_ref[...], lam_init).astype(o_ref.dtype)


def _prompt_attention(lams, sg, qb, kb, vb, *, batch, seq, lam_init):
    m, a = qb.shape
    vd = a // N_HEADS
    tq = ATTN_Q_TILE
    assert seq % tq == 0
    lam_spec = pl.BlockSpec((1, vd // 2), lambda b, h: (0, 0))
    head_spec = pl.BlockSpec((seq, vd), lambda b, h: (b, h))
    vmem = 2 * 4 * seq * vd * 2 + 4 * 2 * tq * seq * 4
    return pl.pallas_call(
        functools.partial(_prompt_attn_body, lam_init=lam_init),
        grid=(batch, N_HEADS),
        in_specs=[lam_spec] * 4 + [pl.BlockSpec((1, vd), lambda b, h: (0, 0)),
                                   head_spec, head_spec, head_spec],
        out_specs=head_spec,
        out_shape=jax.ShapeDtypeStruct((m, a), BF16),
        compiler_params=pltpu.CompilerParams(
            dimension_semantics=("parallel", "parallel"),
            vmem_limit_bytes=_vmem_limit(vmem)),
        name="prompt_diff_attention",
    )(*lams, sg, qb, kb, vb)


def _sample_attn_body(pt_ref, lq1_ref, lk1_ref, lq2_ref, lk2_ref, sg_ref, q_ref, kn_ref, vn_ref,
                      *rest, n_pages, lam_init):
    del pt_ref
    k_pages = rest[:n_pages]
    v_pages = rest[n_pages:2 * n_pages]
    o_ref = rest[2 * n_pages]
    n_groups = SAMPLE_HEAD_GROUPS
    kbf_refs = rest[2 * n_pages + 1:2 * n_pages + 1 + n_groups]
    vbf_refs = rest[2 * n_pages + 1 + n_groups:]
    hg = N_HEADS // n_groups
    page_rows = k_pages[0].shape[0] // n_groups
    past_rows = n_pages * page_rows
    n_rows_k = kbf_refs[0].shape[0]
    n_new, a = q_ref.shape
    vd = a // N_HEADS
    hd = vd // 2

    lam = _diff_lambda(lq1_ref, lk1_ref, lq2_ref, lk2_ref, lam_init)
    qf = q_ref[...].astype(F32)
    kn, vn = kn_ref[...], vn_ref[...]
    n_q = 2 * hg * n_new
    r = lax.broadcasted_iota(jnp.int32, (n_q, vd), 0)
    c = lax.broadcasted_iota(jnp.int32, (n_q, vd), 1)
    q_half_mask = (c // hd) == (r // (hg * n_new))
    col = lax.broadcasted_iota(jnp.int32, (n_q, n_rows_k), 1)
    row = lax.broadcasted_iota(jnp.int32, (n_q, n_rows_k), 0)
    q_head, q_tok = (row // n_new) % hg, row % n_new
    new_col = col - past_rows
    visible = (((col < past_rows) & ((col % hg) == q_head))
               | ((new_col >= 0) & (new_col < hg * n_new) & ((new_col // n_new) == q_head)
                  & ((new_col % n_new) <= q_tok)))
    pad = jnp.zeros((n_rows_k - past_rows - hg * n_new, vd), F32)

    for g in range(n_groups):
        head_cols = [slice(h * vd, (h + 1) * vd) for h in range(g, N_HEADS, n_groups)]
        kbf_ref, vbf_ref = kbf_refs[g], vbf_refs[g]
        group_rows = pl.ds(g, page_rows, stride=n_groups)
        for j in range(n_pages):
            rows = slice(j * page_rows, (j + 1) * page_rows)
            kbf_ref[rows, :] = k_pages[j][group_rows, :].astype(BF16)
            vbf_ref[rows, :] = v_pages[j][group_rows, :].astype(BF16)
        kbf_ref[past_rows:, :] = jnp.concatenate([kn[:, hc] for hc in head_cols] + [pad],
                                                 axis=0).astype(BF16)
        vbf_ref[past_rows:, :] = jnp.concatenate([vn[:, hc] for hc in head_cols] + [pad],
                                                 axis=0).astype(BF16)

        q2 = jnp.concatenate([qf[:, hc] for hc in head_cols] * 2, axis=0)
        q2 = jnp.where(q_half_mask, q2, 0.0).astype(BF16)
        s = lax.dot_general(q2, kbf_ref[...], (((1,), (1,)), ((), ())),
                            preferred_element_type=F32)
        s = jnp.where(visible, s, MASK_VALUE)
        p = jnp.exp(s - jnp.max(s, axis=-1, keepdims=True))
        p = p * (1.0 / jnp.sum(p, axis=-1, keepdims=True))
        attn = (p[:n_q // 2] - lam * p[n_q // 2:]).astype(BF16)
        o = jnp.dot(attn, vbf_ref[...], preferred_element_type=F32)
        o = _subln(o, sg_ref[...], lam_init)
        for i, hc in enumerate(head_cols):
            o_ref[:, hc] = o[i * n_new:(i + 1) * n_new]


def _sample_attention(page_table, lams, sg, q, k_new, v_new, k_pool, v_pool, *, page, first_page,
                      lam_init):
    nb, n_new, a = q.shape
    n_pages = page_table.shape[1]
    vd = a // N_HEADS
    page_rows = page * N_HEADS
    n_keys = n_pages * page + SAMPLE_KEY_PAD
    const = lambda b, pt: (0, 0)
    tok_spec = pl.BlockSpec((None, n_new, a), lambda b, pt: (b, 0, 0))
    page_specs = [pl.BlockSpec((page_rows, vd), lambda b, pt, j=j: (first_page + pt[b, j], 0))
                  for j in range(n_pages)]
    vmem = 2 * 2 * n_pages * page * a * 4 + 2 * n_keys * a * 2 + 8 * 16 * n_keys * 4
    return pl.pallas_call(
        functools.partial(_sample_attn_body, n_pages=n_pages, lam_init=lam_init),
        grid_spec=pltpu.PrefetchScalarGridSpec(
            num_scalar_prefetch=1,
            grid=(nb,),
            in_specs=[pl.BlockSpec((1, vd // 2), const)] * 4 + [pl.BlockSpec((1, vd), const)]
                     + [tok_spec] * 3 + page_specs + page_specs,
            out_specs=tok_spec,
            scratch_shapes=[pltpu.VMEM((n_keys * N_HEADS // SAMPLE_HEAD_GROUPS, vd), BF16)]
                           * (2 * SAMPLE_HEAD_GROUPS),
        ),
        out_shape=jax.ShapeDtypeStruct((nb, n_new, a), F32),
        compiler_params=pltpu.CompilerParams(
            dimension_semantics=("arbitrary",), vmem_limit_bytes=_vmem_limit(vmem)),
        name="sample_diff_attention",
    )(page_table, *lams, sg, q, k_new, v_new, *([k_pool] * n_pages), *([v_pool] * n_pages))


def _pool_project(d, wp_ref, scale_ref, g, cols):
    return jnp.dot(d.astype(BF16), wp_ref[g], preferred_element_type=F32) * scale_ref[:, cols]


def _prompt_pool_tile(i, u_ref, halo_ref, wp_ref, scale_ref, z_ref, ext_ref):
    tm, c = u_ref.shape
    cg = c // len(POOL_WINDOWS)
    ext_ref[0:POOL_HALO, :] = jnp.where(i == 0, 0.0, halo_ref[...])
    ext_ref[POOL_HALO:, :] = u_ref[...]
    pos = i * tm + lax.broadcasted_iota(jnp.int32, (tm, 1), 0)
    for g, w in enumerate(POOL_WINDOWS):
        cols = slice(g * cg, (g + 1) * cg)
        cur = ext_ref[POOL_HALO:, cols]
        win = cur
        for s in range(1, w):
            win = win + ext_ref[POOL_HALO - s:POOL_HALO - s + tm, cols]
        cnt = jnp.minimum(w, pos + 1).astype(F32)
        d = win * (1.0 / cnt) - cur
        z_ref[:, cols] = _pool_project(d, wp_ref, scale_ref, g, cols).astype(z_ref.dtype)


def _sample_pool_body(state_ref, u_ref, wp_ref, scale_ref, z_ref, *, pos0):
    n_state = state_ref.shape[0]
    n_new, _, c = u_ref.shape
    cg = c // len(POOL_WINDOWS)

    def ext(r, cols):
        return state_ref[r, :, cols] if r < n_state else u_ref[r - n_state, :, cols]

    for g, w in enumerate(POOL_WINDOWS):
        cols = slice(g * cg, (g + 1) * cg)
        for t in range(n_new):
            cur = ext(n_state + t, cols)
            win = cur
            for s in range(1, w):
                win = win + ext(n_state + t - s, cols)
            d = win * (1.0 / min(w, pos0 + t + 1)) - cur
            z_ref[t, :, cols] = _pool_project(d, wp_ref, scale_ref, g, cols).astype(z_ref.dtype)


def _sample_pool(state_t, u_t, wp, scale, *, pos0):
    n_new, nb, c = u_t.shape
    vmem = 2 * (state_t.size + u_t.size) * 4 + wp.size * 4 + 4 * nb * c * 4
    return pl.pallas_call(
        functools.partial(_sample_pool_body, pos0=pos0),
        out_shape=jax.ShapeDtypeStruct((n_new, nb, c), BF16),
        compiler_params=pltpu.CompilerParams(vmem_limit_bytes=_vmem_limit(vmem)),
        name="sample_pool_mixer",
    )(state_t, u_t, wp, scale)


def _outproj_body(x_ref, o_ref, z_ref, w_ref, y_ref):
    a = o_ref.shape[1]
    y = x_ref[...] + jnp.dot(o_ref[...].astype(BF16), w_ref[0:a, :], preferred_element_type=F32)
    y_ref[...] = y + jnp.dot(z_ref[...].astype(BF16), w_ref[a:, :], preferred_element_type=F32)


def _outproj(x, o, z, w_out):
    m, d = x.shape
    a, c = o.shape[1], z.shape[1]
    tm = min(PROJ_TOKEN_TILE, m)
    assert m % tm == 0
    row = lambda i: (i, 0)
    vmem = 2 * 2 * tm * d * 4 + 2 * tm * (a + c) * 4 + w_out.size * 2 + tm * d * 4
    return pl.pallas_call(
        _outproj_body,
        grid=(m // tm,),
        in_specs=[
            pl.BlockSpec((tm, d), row), pl.BlockSpec((tm, a), row), pl.BlockSpec((tm, c), row),
            pl.BlockSpec(w_out.shape, lambda i: (0, 0), pipeline_mode=pl.Buffered(1)),
        ],
        out_specs=pl.BlockSpec((tm, d), row),
        out_shape=jax.ShapeDtypeStruct((m, d), F32),
        compiler_params=pltpu.CompilerParams(
            dimension_semantics=("parallel",), vmem_limit_bytes=_vmem_limit(vmem)),
        name="output_projection",
    )(x, o, z, w_out)


def _prompt_outproj_body(x_ref, o_ref, u_ref, halo_ref, wp_ref, scale_ref, w_ref, y_ref,
                         ext_ref, z_ref):
    a = o_ref.shape[1]
    y = x_ref[...] + jnp.dot(o_ref[...], w_ref[0:a, :], preferred_element_type=F32)
    _prompt_pool_tile(pl.program_id(1), u_ref, halo_ref, wp_ref, scale_ref, z_ref, ext_ref)
    y_ref[...] = y + jnp.dot(z_ref[...], w_ref[a:, :], preferred_element_type=F32)


def _prompt_outproj(x, o, u, wp, scale, w_out, *, batch, seq):
    m, d = x.shape
    a, c = o.shape[1], u.shape[1]
    tm = POOL_TOKEN_TILE
    nt = seq // tm
    halo_per_tile = tm // POOL_HALO
    assert seq % tm == 0 and o.dtype == BF16
    row = lambda b, i: (b * nt + i, 0)
    const = lambda b, i: (0, 0)
    vmem = (2 * 2 * tm * d * 4 + 2 * tm * (a * 2 + c * 4) + w_out.size * 2 + wp.size * 2 * 2
            + (tm + POOL_HALO) * c * 4 + tm * c * 2 + tm * d * 4 + 4 * tm * c * 4)
    return pl.pallas_call(
        _prompt_outproj_body,
        grid=(batch, nt),
        in_specs=[
            pl.BlockSpec((tm, d), row), pl.BlockSpec((tm, a), row), pl.BlockSpec((tm, c), row),
            pl.BlockSpec((POOL_HALO, c),
                         lambda b, i: (jnp.maximum((b * nt + i) * halo_per_tile - 1, 0), 0)),
            pl.BlockSpec(wp.shape, lambda b, i: (0, 0, 0)),
            pl.BlockSpec((1, c), const),
            pl.BlockSpec(w_out.shape, const, pipeline_mode=pl.Buffered(1)),
        ],
        out_specs=pl.BlockSpec((tm, d), row),
        out_shape=jax.ShapeDtypeStruct((m, d), F32),
        scratch_shapes=[pltpu.VMEM((tm + POOL_HALO, c), F32), pltpu.VMEM((tm, c), BF16)],
        compiler_params=pltpu.CompilerParams(
            dimension_semantics=("parallel", "arbitrary"), vmem_limit_bytes=_vmem_limit(vmem)),
        name="prompt_pool_output_projection",
    )(x, o, u, u, wp, scale, w_out)


def kernel(x_prompt, x_sample, cache_k, cache_v, state_pool, page_table, ffn1_norm, ffn1_w_gate, ffn1_w_up, ffn1_w_down, mix_norm, w_in, lambda_q1, lambda_k1, lambda_q2, lambda_k2, subln_gain, w_pool, pool_scale, w_out, ffn2_norm, ffn2_w_gate, ffn2_w_up, ffn2_w_down, final_norm):
    batch, seq, d = x_prompt.shape
    nb, n_new, _ = x_sample.shape
    depth, n_phys, page, n_heads, vd = cache_k.shape
    assert n_heads == N_HEADS
    a = n_heads * vd
    c = state_pool.shape[-1]
    past_len = page_table.shape[1] * page
    k_pool = cache_k.reshape(depth * n_phys * page * n_heads, vd)
    v_pool = cache_v.reshape(depth * n_phys * page * n_heads, vd)

    xp = x_prompt.reshape(batch * seq, d)
    xs = x_sample.reshape(nb * n_new, d)
    fg = final_norm.reshape(1, d)
    k_p, v_p, buf_p, k_s, v_s, buf_s = [], [], [], [], [], []
    for l in range(depth):
        lam_init = _lambda_init(l)
        last = l == depth - 1
        mix_g = mix_norm[l].reshape(1, d)
        w_out_b = w_out[l].astype(BF16)
        wp_b = w_pool[l].astype(BF16)
        scale = pool_scale[l].reshape(1, c)
        lams = tuple(p[l].reshape(1, -1) for p in (lambda_q1, lambda_k1, lambda_q2, lambda_k2))
        sg = subln_gain[l].reshape(1, vd)

        xp, xs = _ffn_both(xp, xs, ffn1_norm[l].reshape(1, d), ffn1_w_gate, ffn1_w_up, ffn1_w_down,
                           fg, layer=l, apply_final_norm=False)

        (qs, ks, vs, us, _, _), w_in_b = _inproj_converting(xs, mix_g, w_in, a, layer=l)

        qb, kp, vp, up, kb, vb = _inproj(xp, mix_g, w_in_b, a, layer=0)
        op = _prompt_attention(lams, sg, qb, kb, vb, batch=batch, seq=seq, lam_init=lam_init)
        xp = _prompt_outproj(xp, op, up, wp_b, scale, w_out_b, batch=batch, seq=seq)

        os_ = _sample_attention(
            page_table, lams, sg, qs.reshape(nb, n_new, a), ks.reshape(nb, n_new, a),
            vs.reshape(nb, n_new, a), k_pool, v_pool, page=page,
            first_page=l * n_phys, lam_init=lam_init)
        us3 = us.reshape(nb, n_new, c)
        zs_t = _sample_pool(jnp.transpose(state_pool[l], (1, 0, 2)), jnp.transpose(us3, (1, 0, 2)),
                            wp_b, scale, pos0=past_len)
        zs = jnp.transpose(zs_t, (1, 0, 2)).reshape(nb * n_new, c)
        xs = _outproj(xs, os_.reshape(nb * n_new, a), zs, w_out_b)

        xp, xs = _ffn_both(xp, xs, ffn2_norm[l].reshape(1, d), ffn2_w_gate, ffn2_w_up, ffn2_w_down,
                           fg, layer=l, apply_final_norm=last)

        k_p.append(kp.reshape(batch, seq, n_heads, vd))
        v_p.append(vp.reshape(batch, seq, n_heads, vd))
        buf_p.append(up.reshape(batch, seq, c)[:, seq - POOL_BUF:])
        k_s.append(ks.reshape(nb, n_new, n_heads, vd))
        v_s.append(vs.reshape(nb, n_new, n_heads, vd))
        buf_s.append(jnp.concatenate([state_pool[l], us3], axis=1)[:, -POOL_BUF:])

    return (xp.reshape(batch, seq, d), xs.reshape(nb, n_new, d),
            jnp.stack(k_p), jnp.stack(v_p), jnp.stack(buf_p),
            jnp.stack(k_s), jnp.stack(v_s), jnp.stack(buf_s))
```

```python
import functools
import math

import jax
import jax.numpy as jnp
from jax import lax
from jax.experimental import pallas as pl
from jax.experimental.pallas import tpu as pltpu

F32 = jnp.float32
BF16 = jnp.bfloat16

N_HEADS = 8
POOL_WINDOWS = (2, 4, 8, 16)
POOL_BUF = max(POOL_WINDOWS) - 1
FFN_RESIDUAL = 0.5
NORM_EPS = 1e-6
SUBLN_EPS = 1e-5
MASK_VALUE = -1e30

V7X_VMEM_BYTES = 64 * 1024 * 1024
LANES = 128

FFN_TOKEN_TILE = 512
FFN_HIDDEN_TILE = 1024
FFN_EMIT_HIDDEN_TILE = 256
PROJ_TOKEN_TILE = 512
ATTN_Q_TILE = 256
POOL_TOKEN_TILE = 512
POOL_HALO = 16
SAMPLE_HEAD_GROUPS = 4
SAMPLE_KEY_PAD = 128


def _vmem_limit(nbytes):
    return int(min(nbytes * 5 // 4 + (8 << 20), V7X_VMEM_BYTES - (4 << 20)))


def _rmsnorm(x, g, eps):
    return x * lax.rsqrt(jnp.mean(x * x, axis=-1, keepdims=True) + eps) * g


def _lambda_init(layer):
    return 0.8 - 0.6 * math.exp(-0.3 * layer)


def _diff_lambda(lq1_ref, lk1_ref, lq2_ref, lk2_ref, lam_init):
    a = jnp.sum(lq1_ref[...] * lk1_ref[...], axis=-1, keepdims=True)
    b = jnp.sum(lq2_ref[...] * lk2_ref[...], axis=-1, keepdims=True)
    return jnp.exp(a) - jnp.exp(b) + lam_init


def _swiglu_partial(h, wg, wu, wd):
    gate = jnp.dot(h, wg, preferred_element_type=F32)
    up = jnp.dot(h, wu, preferred_element_type=F32)
    act = (FFN_RESIDUAL * (gate * jax.nn.sigmoid(gate)) * up).astype(BF16)
    return jnp.dot(act, wd, preferred_element_type=F32)


def _ffn_body(x_ref, g_ref, wg_ref, wu_ref, wd_ref, wg_rem_ref, wu_rem_ref, wd_rem_ref, fg_ref,
              o_ref, *rest, apply_final_norm, emit_bf16_weights):
    f = pl.program_id(1)
    h_ref = rest[-1]

    @pl.when(f == 0)
    def _():
        x = x_ref[...]
        h = _rmsnorm(x, g_ref[...], NORM_EPS).astype(BF16)
        h_ref[...] = h
        o_ref[...] = x + _swiglu_partial(h, wg_rem_ref[...], wu_rem_ref[...], wd_rem_ref[...])

    if emit_bf16_weights:
        wg, wu, wd = (r[...].astype(BF16) for r in (wg_ref, wu_ref, wd_ref))
        for out_ref, w in zip(rest[:3], (wg, wu, wd)):
            out_ref[...] = w
    else:
        wg, wu, wd = wg_ref[...], wu_ref[...], wd_ref[...]
    o_ref[...] += _swiglu_partial(h_ref[...], wg, wu, wd)

    if apply_final_norm:
        @pl.when(f == pl.num_programs(1) - 1)
        def _():
            o_ref[...] = _rmsnorm(o_ref[...], fg_ref[...], NORM_EPS)


def _ffn(x, g, wg, wu, wd, w_rem, fg, *, layer, n_main, apply_final_norm,
         emit_bf16_weights=False):
    m, d = x.shape
    wg_rem, wu_rem, wd_rem = w_rem
    rem = wg_rem.shape[1]
    tm = min(FFN_TOKEN_TILE, m)
    tf = FFN_EMIT_HIDDEN_TILE if emit_bf16_weights else FFN_HIDDEN_TILE
    assert m % tm == 0 and n_main % tf == 0 and rem % LANES == 0
    assert wg.dtype == (F32 if emit_bf16_weights else BF16)
    assert not emit_bf16_weights or m == tm
    const = lambda i, f: (0, 0)
    resident = functools.partial(pl.BlockSpec, index_map=const, pipeline_mode=pl.Buffered(1))
    col_tile = pl.BlockSpec((None, d, tf), lambda i, f: (layer, 0, f))
    row_tile = pl.BlockSpec((None, tf, d), lambda i, f: (layer, f, 0))
    tok_tile = pl.BlockSpec((tm, d), lambda i, f: (i, 0))
    out_specs, out_shape = [tok_tile], [jax.ShapeDtypeStruct((m, d), F32)]
    if emit_bf16_weights:
        out_specs += [pl.BlockSpec((None, d, tf), lambda i, f: (0, 0, f))] * 2
        out_specs += [pl.BlockSpec((None, tf, d), lambda i, f: (0, f, 0))]
        out_shape += [jax.ShapeDtypeStruct((1, d, n_main), BF16)] * 2
        out_shape += [jax.ShapeDtypeStruct((1, n_main, d), BF16)]
    w_bytes = wg.dtype.itemsize + (2 if emit_bf16_weights else 0)
    vmem = (2 * 2 * tm * d * 4 + tm * d * 2 + 2 * 3 * d * tf * w_bytes + 3 * d * rem * 2
            + 3 * tm * tf * 4)
    outs = pl.pallas_call(
        functools.partial(_ffn_body, apply_final_norm=apply_final_norm,
                          emit_bf16_weights=emit_bf16_weights),
        grid=(m // tm, n_main // tf),
        in_specs=[
            tok_tile,
            pl.BlockSpec((1, d), const),
            col_tile, col_tile, row_tile,
            resident((d, rem)), resident((d, rem)), resident((rem, d)),
            pl.BlockSpec((1, d), const),
        ],
        out_specs=out_specs,
        out_shape=out_shape,
        scratch_shapes=[pltpu.VMEM((tm, d), BF16)],
        compiler_params=pltpu.CompilerParams(
            dimension_semantics=("parallel", "arbitrary"),
            vmem_limit_bytes=_vmem_limit(vmem)),
        name="ffn_half_step",
    )(x, g, wg, wu, wd, wg_rem, wu_rem, wd_rem, fg)
    return outs if emit_bf16_weights else outs[0]


def _ffn_both(xp, xs, g, wg, wu, wd, fg, *, layer, apply_final_norm):
    d_ff = wg.shape[-1]
    n_main = d_ff // FFN_HIDDEN_TILE * FFN_HIDDEN_TILE
    w_rem = (wg[layer, :, n_main:].astype(BF16), wu[layer, :, n_main:].astype(BF16),
             wd[layer, n_main:, :].astype(BF16))
    xs, wg_b, wu_b, wd_b = _ffn(xs, g, wg, wu, wd, w_rem, fg, layer=layer, n_main=n_main,
                                apply_final_norm=apply_final_norm, emit_bf16_weights=True)
    xp = _ffn(xp, g, wg_b, wu_b, wd_b, w_rem, fg, layer=0, n_main=n_main,
              apply_final_norm=apply_final_norm)
    return xp, xs


def _inproj_body(x_ref, g_ref, w_ref, q_ref, k_ref, v_ref, u_ref, kb_ref, vb_ref, *, q_scale):
    h = _rmsnorm(x_ref[...], g_ref[...], NORM_EPS).astype(BF16)
    a = q_ref.shape[1]
    q_ref[...] = (jnp.dot(h, w_ref[:, 0:a], preferred_element_type=F32) * q_scale).astype(BF16)
    k = jnp.dot(h, w_ref[:, a:2 * a], preferred_element_type=F32)
    k_ref[...] = k
    kb_ref[...] = k.astype(BF16)
    v = jnp.dot(h, w_ref[:, 2 * a:3 * a], preferred_element_type=F32)
    v_ref[...] = v
    vb_ref[...] = v.astype(BF16)
    u_ref[...] = jnp.dot(h, w_ref[:, 3 * a:], preferred_element_type=F32)


def _inproj_outputs(m, a, pw):
    return [jax.ShapeDtypeStruct((m, a), BF16), jax.ShapeDtypeStruct((m, a), F32),
            jax.ShapeDtypeStruct((m, a), F32), jax.ShapeDtypeStruct((m, pw), F32),
            jax.ShapeDtypeStruct((m, a), BF16), jax.ShapeDtypeStruct((m, a), BF16)]


def _inproj_converting_body(x_ref, g_ref, w_ref, q_ref, k_ref, v_ref, u_ref, kb_ref, vb_ref,
                            wb_ref, h_ref, *, q_scale):
    j = pl.program_id(0)

    @pl.when(j == 0)
    def _():
        h_ref[...] = _rmsnorm(x_ref[...], g_ref[...], NORM_EPS).astype(BF16)

    w = w_ref[...].astype(BF16)
    wb_ref[...] = w
    y = jnp.dot(h_ref[...], w, preferred_element_type=F32)

    @pl.when(j == 0)
    def _():
        q_ref[...] = (y * q_scale).astype(BF16)

    @pl.when(j == 1)
    def _():
        k_ref[...] = y
        kb_ref[...] = y.astype(BF16)

    @pl.when(j == 2)
    def _():
        v_ref[...] = y
        vb_ref[...] = y.astype(BF16)

    @pl.when(j == 3)
    def _():
        u_ref[...] = y


def _inproj_converting(x, g, w_in, attn_width, *, layer):
    m, d = x.shape
    n = w_in.shape[-1]
    a = attn_width
    head_dim = a // (2 * N_HEADS)
    assert n == 4 * a and m <= PROJ_TOKEN_TILE
    whole = lambda j: (0, 0)
    vmem = (m * d * 4 + 2 * d * a * (4 + 2) + 2 * m * a * (2 + 4 + 4 + 4 + 2 + 2) + m * d * 2
            + m * a * 4)
    *outs, w_bf16 = pl.pallas_call(
        functools.partial(_inproj_converting_body, q_scale=head_dim ** -0.5),
        grid=(4,),
        in_specs=[
            pl.BlockSpec((m, d), whole, pipeline_mode=pl.Buffered(1)),
            pl.BlockSpec((1, d), whole),
            pl.BlockSpec((None, d, a), lambda j: (layer, 0, j)),
        ],
        out_specs=[pl.BlockSpec((m, a), whole)] * 6
                  + [pl.BlockSpec((None, d, a), lambda j: (0, 0, j))],
        out_shape=_inproj_outputs(m, a, a) + [jax.ShapeDtypeStruct((1, d, n), BF16)],
        scratch_shapes=[pltpu.VMEM((m, d), BF16)],
        compiler_params=pltpu.CompilerParams(
            dimension_semantics=("arbitrary",), vmem_limit_bytes=_vmem_limit(vmem)),
        name="input_projection_converting",
    )(x, g, w_in)
    return outs, w_bf16


def _inproj(x, g, w_in, attn_width, *, layer):
    m, d = x.shape
    n = w_in.shape[-1]
    a = attn_width
    pw = n - 3 * a
    head_dim = a // (2 * N_HEADS)
    tm = min(PROJ_TOKEN_TILE, m)
    assert m % tm == 0
    vmem = 2 * tm * d * 4 + d * n * 2 + 2 * tm * (a * 2 + 2 * a * 6 + pw * 4) + tm * n * 4
    row = lambda i: (i, 0)
    return pl.pallas_call(
        functools.partial(_inproj_body, q_scale=head_dim ** -0.5),
        grid=(m // tm,),
        in_specs=[
            pl.BlockSpec((tm, d), row),
            pl.BlockSpec((1, d), lambda i: (0, 0)),
            pl.BlockSpec((None, d, n), lambda i: (layer, 0, 0), pipeline_mode=pl.Buffered(1)),
        ],
        out_specs=[
            pl.BlockSpec((tm, a), row), pl.BlockSpec((tm, a), row), pl.BlockSpec((tm, a), row),
            pl.BlockSpec((tm, pw), row), pl.BlockSpec((tm, a), row), pl.BlockSpec((tm, a), row),
        ],
        out_shape=_inproj_outputs(m, a, pw),
        compiler_params=pltpu.CompilerParams(
            dimension_semantics=("parallel",), vmem_limit_bytes=_vmem_limit(vmem)),
        name="input_projection",
    )(x, g, w_in)


def _subln(o, gain, lam_init):
    return _rmsnorm(o, gain, SUBLN_EPS) * (1.0 - lam_init)


def _prompt_attn_body(lq1_ref, lk1_ref, lq2_ref, lk2_ref, sg_ref, q_ref, k_ref, v_ref, o_ref, *,
                      lam_init):
    seq, vd = q_ref.shape
    hd = vd // 2
    tq = ATTN_Q_TILE
    nt = (((1,), (1,)), ((), ()))
    lam = _diff_lambda(lq1_ref, lk1_ref, lq2_ref, lk2_ref, lam_init)
    lane = lax.broadcasted_iota(jnp.int32, (tq, vd), 1)
    row = lax.broadcasted_iota(jnp.int32, (2 * tq, tq), 0)
    row = jnp.where(row >= tq, row - tq, row)
    col = lax.broadcasted_iota(jnp.int32, (2 * tq, tq), 1)
    causal = col <= row

    for i in range(seq // tq):
        lo, hi = i * tq, (i + 1) * tq
        q = q_ref[lo:hi, :]
        zero = jnp.zeros_like(q)
        qz = jnp.concatenate([jnp.where(lane < hd, q, zero), jnp.where(lane >= hd, q, zero)],
                             axis=0)
        s_diag = lax.dot_general(qz, k_ref[lo:hi, :], nt, preferred_element_type=F32)
        s_diag = jnp.where(causal, s_diag, MASK_VALUE)
        m = jnp.max(s_diag, axis=-1, keepdims=True)
        if i > 0:
            s_past = lax.dot_general(qz, k_ref[0:lo, :], nt, preferred_element_type=F32)
            m = jnp.maximum(m, jnp.max(s_past, axis=-1, keepdims=True))
        p_diag = jnp.exp(s_diag - m)
        l = jnp.sum(p_diag, axis=-1, keepdims=True)
        acc = jnp.dot(p_diag.astype(BF16), v_ref[lo:hi, :], preferred_element_type=F32)
        if i > 0:
            p_past = jnp.exp(s_past - m)
            l = l + jnp.sum(p_past, axis=-1, keepdims=True)
            acc = acc + jnp.dot(p_past.astype(BF16), v_ref[0:lo, :], preferred_element_type=F32)
        on = acc * (1.0 / l)
        o = on[:tq] - lam * on[tq:]
        o_ref[lo:hi, :] = _subln(o, sg_ref[...], lam_init).astype(o_ref.dtype)


def _prompt_attention(lams, sg, qb, kb, vb, *, batch, seq, lam_init):
    m, a = qb.shape
    vd = a // N_HEADS
    tq = ATTN_Q_TILE
    assert seq % tq == 0
    lam_spec = pl.BlockSpec((1, vd // 2), lambda b, h: (0, 0))
    head_spec = pl.BlockSpec((seq, vd), lambda b, h: (b, h))
    vmem = 2 * 4 * seq * vd * 2 + 4 * 2 * tq * seq * 4
    return pl.pallas_call(
        functools.partial(_prompt_attn_body, lam_init=lam_init),
        grid=(batch, N_HEADS),
        in_specs=[lam_spec] * 4 + [pl.BlockSpec((1, vd), lambda b, h: (0, 0)),
                                   head_spec, head_spec, head_spec],
        out_specs=head_spec,
        out_shape=jax.ShapeDtypeStruct((m, a), BF16),
        compiler_params=pltpu.CompilerParams(
            dimension_semantics=("parallel", "parallel"),
            vmem_limit_bytes=_vmem_limit(vmem)),
        name="prompt_diff_attention",
    )(*lams, sg, qb, kb, vb)


def _sample_attn_body(pt_ref, lq1_ref, lk1_ref, lq2_ref, lk2_ref, sg_ref, q_ref, kn_ref, vn_ref,
                      *rest, n_pages, lam_init):
    del pt_ref
    k_pages = rest[:n_pages]
    v_pages = rest[n_pages:2 * n_pages]
    o_ref = rest[2 * n_pages]
    n_groups = SAMPLE_HEAD_GROUPS
    kbf_refs = rest[2 * n_pages + 1:2 * n_pages + 1 + n_groups]
    vbf_refs = rest[2 * n_pages + 1 + n_groups:]
    hg = N_HEADS // n_groups
    page_rows = k_pages[0].shape[0] // n_groups
    past_rows = n_pages * page_rows
    n_rows_k = kbf_refs[0].shape[0]
    n_new, a = q_ref.shape
    vd = a // N_HEADS
    hd = vd // 2

    lam = _diff_lambda(lq1_ref, lk1_ref, lq2_ref, lk2_ref, lam_init)
    qf = q_ref[...].astype(F32)
    kn, vn = kn_ref[...], vn_ref[...]
    n_q = 2 * hg * n_new
    r = lax.broadcasted_iota(jnp.int32, (n_q, vd), 0)
    c = lax.broadcasted_iota(jnp.int32, (n_q, vd), 1)
    q_half_mask = (c // hd) == (r // (hg * n_new))
    col = lax.broadcasted_iota(jnp.int32, (n_q, n_rows_k), 1)
    row = lax.broadcasted_iota(jnp.int32, (n_q, n_rows_k), 0)
    q_head, q_tok = (row // n_new) % hg, row % n_new
    new_col = col - past_rows
    visible = (((col < past_rows) & ((col % hg) == q_head))
               | ((new_col >= 0) & (new_col < hg * n_new) & ((new_col // n_new) == q_head)
                  & ((new_col % n_new) <= q_tok)))
    pad = jnp.zeros((n_rows_k - past_rows - hg * n_new, vd), F32)

    for g in range(n_groups):
        head_cols = [slice(h * vd, (h + 1) * vd) for h in range(g, N_HEADS, n_groups)]
        kbf_ref, vbf_ref = kbf_refs[g], vbf_refs[g]
        group_rows = pl.ds(g, page_rows, stride=n_groups)
        for j in range(n_pages):
            rows = slice(j * page_rows, (j + 1) * page_rows)
            kbf_ref[rows, :] = k_pages[j][group_rows, :].astype(BF16)
            vbf_ref[rows, :] = v_pages[j][group_rows, :].astype(BF16)
        kbf_ref[past_rows:, :] = jnp.concatenate([kn[:, hc] for hc in head_cols] + [pad],
                                                 axis=0).astype(BF16)
        vbf_ref[past_rows:, :] = jnp.concatenate([vn[:, hc] for hc in head_cols] + [pad],
                                                 axis=0).astype(BF16)

        q2 = jnp.concatenate([qf[:, hc] for hc in head_cols] * 2, axis=0)
        q2 = jnp.where(q_half_mask, q2, 0.0).astype(BF16)
        s = lax.dot_general(q2, kbf_ref[...], (((1,), (1,)), ((), ())),
                            preferred_element_type=F32)
        s = jnp.where(visible, s, MASK_VALUE)
        p = jnp.exp(s - jnp.max(s, axis=-1, keepdims=True))
        p = p * (1.0 / jnp.sum(p, axis=-1, keepdims=True))
        attn = (p[:n_q // 2] - lam * p[n_q // 2:]).astype(BF16)
        o = jnp.dot(attn, vbf_ref[...], preferred_element_type=F32)
        o = _subln(o, sg_ref[...], lam_init)
        for i, hc in enumerate(head_cols):
            o_ref[:, hc] = o[i * n_new:(i + 1) * n_new]


def _sample_attention(page_table, lams, sg, q, k_new, v_new, k_pool, v_pool, *, page, first_page,
                      lam_init):
    nb, n_new, a = q.shape
    n_pages = page_table.shape[1]
    vd = a // N_HEADS
    page_rows = page * N_HEADS
    n_keys = n_pages * page + SAMPLE_KEY_PAD
    const = lambda b, pt: (0, 0)
    tok_spec = pl.BlockSpec((None, n_new, a), lambda b, pt: (b, 0, 0))
    page_specs = [pl.BlockSpec((page_rows, vd), lambda b, pt, j=j: (first_page + pt[b, j], 0))
                  for j in range(n_pages)]
    vmem = 2 * 2 * n_pages * page * a * 4 + 2 * n_keys * a * 2 + 8 * 16 * n_keys * 4
    return pl.pallas_call(
        functools.partial(_sample_attn_body, n_pages=n_pages, lam_init=lam_init),
        grid_spec=pltpu.PrefetchScalarGridSpec(
            num_scalar_prefetch=1,
            grid=(nb,),
            in_specs=[pl.BlockSpec((1, vd // 2), const)] * 4 + [pl.BlockSpec((1, vd), const)]
                     + [tok_spec] * 3 + page_specs + page_specs,
            out_specs=tok_spec,
            scratch_shapes=[pltpu.VMEM((n_keys * N_HEADS // SAMPLE_HEAD_GROUPS, vd), BF16)]
                           * (2 * SAMPLE_HEAD_GROUPS),
        ),
        out_shape=jax.ShapeDtypeStruct((nb, n_new, a), F32),
        compiler_params=pltpu.CompilerParams(
            dimension_semantics=("arbitrary",), vmem_limit_bytes=_vmem_limit(vmem)),
        name="sample_diff_attention",
    )(page_table, *lams, sg, q, k_new, v_new, *([k_pool] * n_pages), *([v_pool] * n_pages))


def _pool_project(d, wp_ref, scale_ref, g, cols):
    return jnp.dot(d.astype(BF16), wp_ref[g], preferred_element_type=F32) * scale_ref[:, cols]


def _prompt_pool_tile(i, u_ref, halo_ref, wp_ref, scale_ref, z_ref, ext_ref):
    tm, c = u_ref.shape
    cg = c // len(POOL_WINDOWS)
    ext_ref[0:POOL_HALO, :] = jnp.where(i == 0, 0.0, halo_ref[...])
    ext_ref[POOL_HALO:, :] = u_ref[...]
    pos = i * tm + lax.broadcasted_iota(jnp.int32, (tm, 1), 0)
    for g, w in enumerate(POOL_WINDOWS):
        cols = slice(g * cg, (g + 1) * cg)
        cur = ext_ref[POOL_HALO:, cols]
        win = cur
        for s in range(1, w):
            win = win + ext_ref[POOL_HALO - s:POOL_HALO - s + tm, cols]
        cnt = jnp.minimum(w, pos + 1).astype(F32)
        d = win * (1.0 / cnt) - cur
        z_ref[:, cols] = _pool_project(d, wp_ref, scale_ref, g, cols).astype(z_ref.dtype)


def _sample_pool_body(state_ref, u_ref, wp_ref, scale_ref, z_ref, *, pos0):
    n_state = state_ref.shape[0]
    n_new, _, c = u_ref.shape
    cg = c // len(POOL_WINDOWS)

    def ext(r, cols):
        return state_ref[r, :, cols] if r < n_state else u_ref[r - n_state, :, cols]

    for g, w in enumerate(POOL_WINDOWS):
        cols = slice(g * cg, (g + 1) * cg)
        for t in range(n_new):
            cur = ext(n_state + t, cols)
            win = cur
            for s in range(1, w):
                win = win + ext(n_state + t - s, cols)
            d = win * (1.0 / min(w, pos0 + t + 1)) - cur
            z_ref[t, :, cols] = _pool_project(d, wp_ref, scale_ref, g, cols).astype(z_ref.dtype)


def _sample_pool(state_t, u_t, wp, scale, *, pos0):
    n_new, nb, c = u_t.shape
    vmem = 2 * (state_t.size + u_t.size) * 4 + wp.size * 4 + 4 * nb * c * 4
    return pl.pallas_call(
        functools.partial(_sample_pool_body, pos0=pos0),
        out_shape=jax.ShapeDtypeStruct((n_new, nb, c), BF16),
        compiler_params=pltpu.CompilerParams(vmem_limit_bytes=_vmem_limit(vmem)),
        name="sample_pool_mixer",
    )(state_t, u_t, wp, scale)


def _outproj_body(x_ref, o_ref, z_ref, w_ref, y_ref):
    a = o_ref.shape[1]
    y = x_ref[...] + jnp.dot(o_ref[...].astype(BF16), w_ref[0:a, :], preferred_element_type=F32)
    y_ref[...] = y + jnp.dot(z_ref[...].astype(BF16), w_ref[a:, :], preferred_element_type=F32)


def _outproj(x, o, z, w_out):
    m, d = x.shape
    a, c = o.shape[1], z.shape[1]
    tm = min(PROJ_TOKEN_TILE, m)
    assert m % tm == 0
    row = lambda i: (i, 0)
    vmem = 2 * 2 * tm * d * 4 + 2 * tm * (a + c) * 4 + w_out.size * 2 + tm * d * 4
    return pl.pallas_call(
        _outproj_body,
        grid=(m // tm,),
        in_specs=[
            pl.BlockSpec((tm, d), row), pl.BlockSpec((tm, a), row), pl.BlockSpec((tm, c), row),
            pl.BlockSpec(w_out.shape, lambda i: (0, 0), pipeline_mode=pl.Buffered(1)),
        ],
        out_specs=pl.BlockSpec((tm, d), row),
        out_shape=jax.ShapeDtypeStruct((m, d), F32),
        compiler_params=pltpu.CompilerParams(
            dimension_semantics=("parallel",), vmem_limit_bytes=_vmem_limit(vmem)),
        name="output_projection",
    )(x, o, z, w_out)


def _prompt_outproj_body(x_ref, o_ref, u_ref, halo_ref, wp_ref, scale_ref, w_ref, y_ref,
                         ext_ref, z_ref):
    a = o_ref.shape[1]
    y = x_ref[...] + jnp.dot(o_ref[...], w_ref[0:a, :], preferred_element_type=F32)
    _prompt_pool_tile(pl.program_id(1), u_ref, halo_ref, wp_ref, scale_ref, z_ref, ext_ref)
    y_ref[...] = y + jnp.dot(z_ref[...], w_ref[a:, :], preferred_element_type=F32)


def _prompt_outproj(x, o, u, wp, scale, w_out, *, batch, seq):
    m, d = x.shape
    a, c = o.shape[1], u.shape[1]
    tm = POOL_TOKEN_TILE
    nt = seq // tm
    halo_per_tile = tm // POOL_HALO
    assert seq % tm == 0 and o.dtype == BF16
    row = lambda b, i: (b * nt + i, 0)
    const = lambda b, i: (0, 0)
    vmem = (2 * 2 * tm * d * 4 + 2 * tm * (a * 2 + c * 4) + w_out.size * 2 + wp.size * 2 * 2
            + (tm + POOL_HALO) * c * 4 + tm * c * 2 + tm * d * 4 + 4 * tm * c * 4)
    return pl.pallas_call(
        _prompt_outproj_body,
        grid=(batch, nt),
        in_specs=[
            pl.BlockSpec((tm, d), row), pl.BlockSpec((tm, a), row), pl.BlockSpec((tm, c), row),
            pl.BlockSpec((POOL_HALO, c),
                         lambda b, i: (jnp.maximum((b * nt + i) * halo_per_tile - 1, 0), 0)),
            pl.BlockSpec(wp.shape, lambda b, i: (0, 0, 0)),
            pl.BlockSpec((1, c), const),
            pl.BlockSpec(w_out.shape, const, pipeline_mode=pl.Buffered(1)),
        ],
        out_specs=pl.BlockSpec((tm, d), row),
        out_shape=jax.ShapeDtypeStruct((m, d), F32),
        scratch_shapes=[pltpu.VMEM((tm + POOL_HALO, c), F32), pltpu.VMEM((tm, c), BF16)],
        compiler_params=pltpu.CompilerParams(
            dimension_semantics=("parallel", "arbitrary"), vmem_limit_bytes=_vmem_limit(vmem)),
        name="prompt_pool_output_projection",
    )(x, o, u, u, wp, scale, w_out)


def kernel(x_prompt, x_sample, cache_k, cache_v, state_pool, page_table, ffn1_norm, ffn1_w_gate, ffn1_w_up, ffn1_w_down, mix_norm, w_in, lambda_q1, lambda_k1, lambda_q2, lambda_k2, subln_gain, w_pool, pool_scale, w_out, ffn2_norm, ffn2_w_gate, ffn2_w_up, ffn2_w_down, final_norm):
    batch, seq, d = x_prompt.shape
    nb, n_new, _ = x_sample.shape
    depth, n_phys, page, n_heads, vd = cache_k.shape
    assert n_heads == N_HEADS
    a = n_heads * vd
    c = state_pool.shape[-1]
    past_len = page_table.shape[1] * page
    k_pool = cache_k.reshape(depth * n_phys * page * n_heads, vd)
    v_pool = cache_v.reshape(depth * n_phys * page * n_heads, vd)

    xp = x_prompt.reshape(batch * seq, d)
    xs = x_sample.reshape(nb * n_new, d)
    fg = final_norm.reshape(1, d)
    k_p, v_p, buf_p, k_s, v_s, buf_s = [], [], [], [], [], []
    for l in range(depth):
        lam_init = _lambda_init(l)
        last = l == depth - 1
        mix_g = mix_norm[l].reshape(1, d)
        w_out_b = w_out[l].astype(BF16)
        wp_b = w_pool[l].astype(BF16)
        scale = pool_scale[l].reshape(1, c)
        lams = tuple(p[l].reshape(1, -1) for p in (lambda_q1, lambda_k1, lambda_q2, lambda_k2))
        sg = subln_gain[l].reshape(1, vd)

        xp, xs = _ffn_both(xp, xs, ffn1_norm[l].reshape(1, d), ffn1_w_gate, ffn1_w_up, ffn1_w_down,
                           fg, layer=l, apply_final_norm=False)

        (qs, ks, vs, us, _, _), w_in_b = _inproj_converting(xs, mix_g, w_in, a, layer=l)

        qb, kp, vp, up, kb, vb = _inproj(xp, mix_g, w_in_b, a, layer=0)
        op = _prompt_attention(lams, sg, qb, kb, vb, batch=batch, seq=seq, lam_init=lam_init)
        xp = _prompt_outproj(xp, op, up, wp_b, scale, w_out_b, batch=batch, seq=seq)

        os_ = _sample_attention(
            page_table, lams, sg, qs.reshape(nb, n_new, a), ks.reshape(nb, n_new, a),
            vs.reshape(nb, n_new, a), k_pool, v_pool, page=page,
            first_page=l * n_phys, lam_init=lam_init)
        us3 = us.reshape(nb, n_new, c)
        zs_t = _sample_pool(jnp.transpose(state_pool[l], (1, 0, 2)), jnp.transpose(us3, (1, 0, 2)),
                            wp_b, scale, pos0=past_len)
        zs = jnp.transpose(zs_t, (1, 0, 2)).reshape(nb * n_new, c)
        xs = _outproj(xs, os_.reshape(nb * n_new, a), zs, w_out_b)

        xp, xs = _ffn_both(xp, xs, ffn2_norm[l].reshape(1, d), ffn2_w_gate, ffn2_w_up, ffn2_w_down,
                           fg, layer=l, apply_final_norm=last)

        k_p.append(kp.reshape(batch, seq, n_heads, vd))
        v_p.append(vp.reshape(batch, seq, n_heads, vd))
        buf_p.append(up.reshape(batch, seq, c)[:, seq - POOL_BUF:])
        k_s.append(ks.reshape(nb, n_new, n_heads, vd))
        v_s.append(vs.reshape(nb, n_new, n_heads, vd))
        buf_s.append(jnp.concatenate([state_pool[l], us3], axis=1)[:, -POOL_BUF:])

    return (xp.reshape(batch, seq, d), xs.reshape(nb, n_new, d),
            jnp.stack(k_p), jnp.stack(v_p), jnp.stack(buf_p),
            jnp.stack(k_s), jnp.stack(v_s), jnp.stack(buf_s))
```

```python
import functools
import math

import jax
import jax.numpy as jnp
from jax import lax
from jax.experimental import pallas as pl
from jax.experimental.pallas import tpu as pltpu

F32 = jnp.float32
BF16 = jnp.bfloat16

N_HEADS = 8
POOL_WINDOWS = (2, 4, 8, 16)
POOL_BUF = max(POOL_WINDOWS) - 1
FFN_RESIDUAL = 0.5
NORM_EPS = 1e-6
SUBLN_EPS = 1e-5
MASK_VALUE = -1e30

V7X_VMEM_BYTES = 64 * 1024 * 1024
LANES = 128

FFN_TOKEN_TILE = 512
FFN_HIDDEN_TILE = 1024
FFN_EMIT_HIDDEN_TILE = 256
PROJ_TOKEN_TILE = 512
ATTN_Q_TILE = 256
POOL_TOKEN_TILE = 512
POOL_HALO = 16
SAMPLE_HEAD_GROUPS = 2
SAMPLE_KEY_PAD = 32


def _vmem_limit(nbytes):
    return int(min(nbytes * 5 // 4 + (8 << 20), V7X_VMEM_BYTES - (4 << 20)))


def _rmsnorm(x, g, eps):
    return x * lax.rsqrt(jnp.mean(x * x, axis=-1, keepdims=True) + eps) * g


def _lambda_init(layer):
    return 0.8 - 0.6 * math.exp(-0.3 * layer)


def _diff_lambda(lq1_ref, lk1_ref, lq2_ref, lk2_ref, lam_init):
    a = jnp.sum(lq1_ref[...] * lk1_ref[...], axis=-1, keepdims=True)
    b = jnp.sum(lq2_ref[...] * lk2_ref[...], axis=-1, keepdims=True)
    return jnp.exp(a) - jnp.exp(b) + lam_init


def _swiglu_partial(h, wg, wu, wd):
    gate = jnp.dot(h, wg, preferred_element_type=F32)
    up = jnp.dot(h, wu, preferred_element_type=F32)
    act = (FFN_RESIDUAL * (gate * jax.nn.sigmoid(gate)) * up).astype(BF16)
    return jnp.dot(act, wd, preferred_element_type=F32)


def _ffn_body(x_ref, g_ref, wg_ref, wu_ref, wd_ref, wg_rem_ref, wu_rem_ref, wd_rem_ref, fg_ref,
              o_ref, *rest, apply_final_norm, emit_bf16_weights):
    f = pl.program_id(1)
    h_ref = rest[-1]

    @pl.when(f == 0)
    def _():
        x = x_ref[...]
        h = _rmsnorm(x, g_ref[...], NORM_EPS).astype(BF16)
        h_ref[...] = h
        o_ref[...] = x + _swiglu_partial(h, wg_rem_ref[...], wu_rem_ref[...], wd_rem_ref[...])

    if emit_bf16_weights:
        wg, wu, wd = (r[...].astype(BF16) for r in (wg_ref, wu_ref, wd_ref))
        for out_ref, w in zip(rest[:3], (wg, wu, wd)):
            out_ref[...] = w
    else:
        wg, wu, wd = wg_ref[...], wu_ref[...], wd_ref[...]
    o_ref[...] += _swiglu_partial(h_ref[...], wg, wu, wd)

    if apply_final_norm:
        @pl.when(f == pl.num_programs(1) - 1)
        def _():
            o_ref[...] = _rmsnorm(o_ref[...], fg_ref[...], NORM_EPS)


def _ffn(x, g, wg, wu, wd, w_rem, fg, *, layer, n_main, apply_final_norm,
         emit_bf16_weights=False):
    m, d = x.shape
    wg_rem, wu_rem, wd_rem = w_rem
    rem = wg_rem.shape[1]
    tm = min(FFN_TOKEN_TILE, m)
    tf = FFN_EMIT_HIDDEN_TILE if emit_bf16_weights else FFN_HIDDEN_TILE
    assert m % tm == 0 and n_main % tf == 0 and rem % LANES == 0
    assert wg.dtype == (F32 if emit_bf16_weights else BF16)
    assert not emit_bf16_weights or m == tm
    const = lambda i, f: (0, 0)
    resident = functools.partial(pl.BlockSpec, index_map=const, pipeline_mode=pl.Buffered(1))
    col_tile = pl.BlockSpec((None, d, tf), lambda i, f: (layer, 0, f))
    row_tile = pl.BlockSpec((None, tf, d), lambda i, f: (layer, f, 0))
    tok_tile = pl.BlockSpec((tm, d), lambda i, f: (i, 0))
    out_specs, out_shape = [tok_tile], [jax.ShapeDtypeStruct((m, d), F32)]
    if emit_bf16_weights:
        out_specs += [pl.BlockSpec((None, d, tf), lambda i, f: (0, 0, f))] * 2
        out_specs += [pl.BlockSpec((None, tf, d), lambda i, f: (0, f, 0))]
        out_shape += [jax.ShapeDtypeStruct((1, d, n_main), BF16)] * 2
        out_shape += [jax.ShapeDtypeStruct((1, n_main, d), BF16)]
    w_bytes = wg.dtype.itemsize + (2 if emit_bf16_weights else 0)
    vmem = (2 * 2 * tm * d * 4 + tm * d * 2 + 2 * 3 * d * tf * w_bytes + 3 * d * rem * 2
            + 3 * tm * tf * 4)
    outs = pl.pallas_call(
        functools.partial(_ffn_body, apply_final_norm=apply_final_norm,
                          emit_bf16_weights=emit_bf16_weights),
        grid=(m // tm, n_main // tf),
        in_specs=[
            tok_tile,
            pl.BlockSpec((1, d), const),
            col_tile, col_tile, row_tile,
            resident((d, rem)), resident((d, rem)), resident((rem, d)),
            pl.BlockSpec((1, d), const),
        ],
        out_specs=out_specs,
        out_shape=out_shape,
        scratch_shapes=[pltpu.VMEM((tm, d), BF16)],
        compiler_params=pltpu.CompilerParams(
            dimension_semantics=("parallel", "arbitrary"),
            vmem_limit_bytes=_vmem_limit(vmem)),
        name="ffn_half_step",
    )(x, g, wg, wu, wd, wg_rem, wu_rem, wd_rem, fg)
    return outs if emit_bf16_weights else outs[0]


def _ffn_both(xp, xs, g, wg, wu, wd, fg, *, layer, apply_final_norm):
    d_ff = wg.shape[-1]
    n_main = d_ff // FFN_HIDDEN_TILE * FFN_HIDDEN_TILE
    w_rem = (wg[layer, :, n_main:].astype(BF16), wu[layer, :, n_main:].astype(BF16),
             wd[layer, n_main:, :].astype(BF16))
    xs, wg_b, wu_b, wd_b = _ffn(xs, g, wg, wu, wd, w_rem, fg, layer=layer, n_main=n_main,
                                apply_final_norm=apply_final_norm, emit_bf16_weights=True)
    xp = _ffn(xp, g, wg_b, wu_b, wd_b, w_rem, fg, layer=0, n_main=n_main,
              apply_final_norm=apply_final_norm)
    return xp, xs


def _inproj_body(x_ref, g_ref, w_ref, q_ref, k_ref, v_ref, u_ref, kb_ref, vb_ref, *, q_scale):
    h = _rmsnorm(x_ref[...], g_ref[...], NORM_EPS).astype(BF16)
    a = q_ref.shape[1]
    q_ref[...] = (jnp.dot(h, w_ref[:, 0:a], preferred_element_type=F32) * q_scale).astype(BF16)
    k = jnp.dot(h, w_ref[:, a:2 * a], preferred_element_type=F32)
    k_ref[...] = k
    kb_ref[...] = k.astype(BF16)
    v = jnp.dot(h, w_ref[:, 2 * a:3 * a], preferred_element_type=F32)
    v_ref[...] = v
    vb_ref[...] = v.astype(BF16)
    u_ref[...] = jnp.dot(h, w_ref[:, 3 * a:], preferred_element_type=F32)


def _inproj_outputs(m, a, pw):
    return [jax.ShapeDtypeStruct((m, a), BF16), jax.ShapeDtypeStruct((m, a), F32),
            jax.ShapeDtypeStruct((m, a), F32), jax.ShapeDtypeStruct((m, pw), F32),
            jax.ShapeDtypeStruct((m, a), BF16), jax.ShapeDtypeStruct((m, a), BF16)]


def _inproj_converting_body(x_ref, g_ref, w_ref, q_ref, k_ref, v_ref, u_ref, kb_ref, vb_ref,
                            wb_ref, h_ref, *, q_scale):
    j = pl.program_id(0)

    @pl.when(j == 0)
    def _():
        h_ref[...] = _rmsnorm(x_ref[...], g_ref[...], NORM_EPS).astype(BF16)

    w = w_ref[...].astype(BF16)
    wb_ref[...] = w
    y = jnp.dot(h_ref[...], w, preferred_element_type=F32)

    @pl.when(j == 0)
    def _():
        q_ref[...] = (y * q_scale).astype(BF16)

    @pl.when(j == 1)
    def _():
        k_ref[...] = y
        kb_ref[...] = y.astype(BF16)

    @pl.when(j == 2)
    def _():
        v_ref[...] = y
        vb_ref[...] = y.astype(BF16)

    @pl.when(j == 3)
    def _():
        u_ref[...] = y


def _inproj_converting(x, g, w_in, attn_width, *, layer):
    m, d = x.shape
    n = w_in.shape[-1]
    a = attn_width
    head_dim = a // (2 * N_HEADS)
    assert n == 4 * a and m <= PROJ_TOKEN_TILE
    whole = lambda j: (0, 0)
    vmem = (m * d * 4 + 2 * d * a * (4 + 2) + 2 * m * a * (2 + 4 + 4 + 4 + 2 + 2) + m * d * 2
            + m * a * 4)
    *outs, w_bf16 = pl.pallas_call(
        functools.partial(_inproj_converting_body, q_scale=head_dim ** -0.5),
        grid=(4,),
        in_specs=[
            pl.BlockSpec((m, d), whole, pipeline_mode=pl.Buffered(1)),
            pl.BlockSpec((1, d), whole),
            pl.BlockSpec((None, d, a), lambda j: (layer, 0, j)),
        ],
        out_specs=[pl.BlockSpec((m, a), whole)] * 6
                  + [pl.BlockSpec((None, d, a), lambda j: (0, 0, j))],
        out_shape=_inproj_outputs(m, a, a) + [jax.ShapeDtypeStruct((1, d, n), BF16)],
        scratch_shapes=[pltpu.VMEM((m, d), BF16)],
        compiler_params=pltpu.CompilerParams(
            dimension_semantics=("arbitrary",), vmem_limit_bytes=_vmem_limit(vmem)),
        name="input_projection_converting",
    )(x, g, w_in)
    return outs, w_bf16


def _inproj(x, g, w_in, attn_width, *, layer):
    m, d = x.shape
    n = w_in.shape[-1]
    a = attn_width
    pw = n - 3 * a
    head_dim = a // (2 * N_HEADS)
    tm = min(PROJ_TOKEN_TILE, m)
    assert m % tm == 0
    vmem = 2 * tm * d * 4 + d * n * 2 + 2 * tm * (a * 2 + 2 * a * 6 + pw * 4) + tm * n * 4
    row = lambda i: (i, 0)
    return pl.pallas_call(
        functools.partial(_inproj_body, q_scale=head_dim ** -0.5),
        grid=(m // tm,),
        in_specs=[
            pl.BlockSpec((tm, d), row),
            pl.BlockSpec((1, d), lambda i: (0, 0)),
            pl.BlockSpec((None, d, n), lambda i: (layer, 0, 0), pipeline_mode=pl.Buffered(1)),
        ],
        out_specs=[
            pl.BlockSpec((tm, a), row), pl.BlockSpec((tm, a), row), pl.BlockSpec((tm, a), row),
            pl.BlockSpec((tm, pw), row), pl.BlockSpec((tm, a), row), pl.BlockSpec((tm, a), row),
        ],
        out_shape=_inproj_outputs(m, a, pw),
        compiler_params=pltpu.CompilerParams(
            dimension_semantics=("parallel",), vmem_limit_bytes=_vmem_limit(vmem)),
        name="input_projection",
    )(x, g, w_in)


def _subln(o, gain, lam_init):
    return _rmsnorm(o, gain, SUBLN_EPS) * (1.0 - lam_init)


def _prompt_attn_body(lq1_ref, lk1_ref, lq2_ref, lk2_ref, sg_ref, q_ref, k_ref, v_ref, o_ref, *,
                      lam_init):
    seq, vd = q_ref.shape
    hd = vd // 2
    tq = ATTN_Q_TILE
    nt = (((1,), (1,)), ((), ()))
    lam = _diff_lambda(lq1_ref, lk1_ref, lq2_ref, lk2_ref, lam_init)
    lane = lax.broadcasted_iota(jnp.int32, (tq, vd), 1)
    row = lax.broadcasted_iota(jnp.int32, (2 * tq, tq), 0)
    row = jnp.where(row >= tq, row - tq, row)
    col = lax.broadcasted_iota(jnp.int32, (2 * tq, tq), 1)
    causal = col <= row

    for i in range(seq // tq):
        lo, hi = i * tq, (i + 1) * tq
        q = q_ref[lo:hi, :]
        zero = jnp.zeros_like(q)
        qz = jnp.concatenate([jnp.where(lane < hd, q, zero), jnp.where(lane >= hd, q, zero)],
                             axis=0)
        s_diag = lax.dot_general(qz, k_ref[lo:hi, :], nt, preferred_element_type=F32)
        s_diag = jnp.where(causal, s_diag, MASK_VALUE)
        m = jnp.max(s_diag, axis=-1, keepdims=True)
        if i > 0:
            s_past = lax.dot_general(qz, k_ref[0:lo, :], nt, preferred_element_type=F32)
            m = jnp.maximum(m, jnp.max(s_past, axis=-1, keepdims=True))
        p_diag = jnp.exp(s_diag - m)
        l = jnp.sum(p_diag, axis=-1, keepdims=True)
        acc = jnp.dot(p_diag.astype(BF16), v_ref[lo:hi, :], preferred_element_type=F32)
        if i > 0:
            p_past = jnp.exp(s_past - m)
            l = l + jnp.sum(p_past, axis=-1, keepdims=True)
            acc = acc + jnp.dot(p_past.astype(BF16), v_ref[0:lo, :], preferred_element_type=F32)
        on = acc * (1.0 / l)
        o = on[:tq] - lam * on[tq:]
        o_ref[lo:hi, :] = _subln(o, sg_ref[...], lam_init).astype(o_ref.dtype)


def _prompt_attention(lams, sg, qb, kb, vb, *, batch, seq, lam_init):
    m, a = qb.shape
    vd = a // N_HEADS
    tq = ATTN_Q_TILE
    assert seq % tq == 0
    lam_spec = pl.BlockSpec((1, vd // 2), lambda b, h: (0, 0))
    head_spec = pl.BlockSpec((seq, vd), lambda b, h: (b, h))
    vmem = 2 * 4 * seq * vd * 2 + 4 * 2 * tq * seq * 4
    return pl.pallas_call(
        functools.partial(_prompt_attn_body, lam_init=lam_init),
        grid=(batch, N_HEADS),
        in_specs=[lam_spec] * 4 + [pl.BlockSpec((1, vd), lambda b, h: (0, 0)),
                                   head_spec, head_spec, head_spec],
        out_specs=head_spec,
        out_shape=jax.ShapeDtypeStruct((m, a), BF16),
        compiler_params=pltpu.CompilerParams(
            dimension_semantics=("parallel", "parallel"),
            vmem_limit_bytes=_vmem_limit(vmem)),
        name="prompt_diff_attention",
    )(*lams, sg, qb, kb, vb)


def _sample_attn_body(pt_ref, lq1_ref, lk1_ref, lq2_ref, lk2_ref, sg_ref, q_ref, kn_ref, vn_ref,
                      *rest, n_pages, lam_init):
    del pt_ref
    k_pages = rest[:n_pages]
    v_pages = rest[n_pages:2 * n_pages]
    o_ref = rest[2 * n_pages]
    n_groups = SAMPLE_HEAD_GROUPS
    kbf_refs = rest[2 * n_pages + 1:2 * n_pages + 1 + n_groups]
    vbf_refs = rest[2 * n_pages + 1 + n_groups:]
    hg = N_HEADS // n_groups
    page_rows = k_pages[0].shape[0] // n_groups
    past_rows = n_pages * page_rows
    n_rows_k = kbf_refs[0].shape[0]
    n_new, a = q_ref.shape
    vd = a // N_HEADS
    hd = vd // 2

    lam = _diff_lambda(lq1_ref, lk1_ref, lq2_ref, lk2_ref, lam_init)
    qf = q_ref[...].astype(F32)
    kn, vn = kn_ref[...], vn_ref[...]
    n_q = 2 * hg * n_new
    r = lax.broadcasted_iota(jnp.int32, (n_q, vd), 0)
    c = lax.broadcasted_iota(jnp.int32, (n_q, vd), 1)
    q_half_mask = (c // hd) == (r // (hg * n_new))
    col = lax.broadcasted_iota(jnp.int32, (n_q, n_rows_k), 1)
    row = lax.broadcasted_iota(jnp.int32, (n_q, n_rows_k), 0)
    q_head, q_tok = (row // n_new) % hg, row % n_new
    new_col = col - past_rows
    visible = (((col < past_rows) & ((col % hg) == q_head))
               | ((new_col >= 0) & (new_col < hg * n_new) & ((new_col // n_new) == q_head)
                  & ((new_col % n_new) <= q_tok)))
    pad = jnp.zeros((n_rows_k - past_rows - hg * n_new, vd), F32)

    for g in range(n_groups):
        head_cols = [slice(h * vd, (h + 1) * vd) for h in range(g, N_HEADS, n_groups)]
        kbf_ref, vbf_ref = kbf_refs[g], vbf_refs[g]
        group_rows = pl.ds(g, page_rows, stride=n_groups)
        for j in range(n_pages):
            rows = slice(j * page_rows, (j + 1) * page_rows)
            kbf_ref[rows, :] = k_pages[j][group_rows, :].astype(BF16)
            vbf_ref[rows, :] = v_pages[j][group_rows, :].astype(BF16)
        kbf_ref[past_rows:, :] = jnp.concatenate([kn[:, hc] for hc in head_cols] + [pad],
                                                 axis=0).astype(BF16)
        vbf_ref[past_rows:, :] = jnp.concatenate([vn[:, hc] for hc in head_cols] + [pad],
                                                 axis=0).astype(BF16)

        q2 = jnp.concatenate([qf[:, hc] for hc in head_cols] * 2, axis=0)
        q2 = jnp.where(q_half_mask, q2, 0.0).astype(BF16)
        s = lax.dot_general(q2, kbf_ref[...], (((1,), (1,)), ((), ())),
                            preferred_element_type=F32)
        s = jnp.where(visible, s, MASK_VALUE)
        p = jnp.exp(s - jnp.max(s, axis=-1, keepdims=True))
        p = p * (1.0 / jnp.sum(p, axis=-1, keepdims=True))
        attn = (p[:n_q // 2] - lam * p[n_q // 2:]).astype(BF16)
        o = jnp.dot(attn, vbf_ref[...], preferred_element_type=F32)
        o = _subln(o, sg_ref[...], lam_init)
        for i, hc in enumerate(head_cols):
            o_ref[:, hc] = o[i * n_new:(i + 1) * n_new]


def _sample_attention(page_table, lams, sg, q, k_new, v_new, k_pool, v_pool, *, page, first_page,
                      lam_init):
    nb, n_new, a = q.shape
    n_pages = page_table.shape[1]
    vd = a // N_HEADS
    page_rows = page * N_HEADS
    n_keys = n_pages * page + SAMPLE_KEY_PAD
    const = lambda b, pt: (0, 0)
    tok_spec = pl.BlockSpec((None, n_new, a), lambda b, pt: (b, 0, 0))
    page_specs = [pl.BlockSpec((page_rows, vd), lambda b, pt, j=j: (first_page + pt[b, j], 0))
                  for j in range(n_pages)]
    vmem = 2 * 2 * n_pages * page * a * 4 + 2 * n_keys * a * 2 + 8 * 16 * n_keys * 4
    return pl.pallas_call(
        functools.partial(_sample_attn_body, n_pages=n_pages, lam_init=lam_init),
        grid_spec=pltpu.PrefetchScalarGridSpec(
            num_scalar_prefetch=1,
            grid=(nb,),
            in_specs=[pl.BlockSpec((1, vd // 2), const)] * 4 + [pl.BlockSpec((1, vd), const)]
                     + [tok_spec] * 3 + page_specs + page_specs,
            out_specs=tok_spec,
            scratch_shapes=[pltpu.VMEM((n_keys * N_HEADS // SAMPLE_HEAD_GROUPS, vd), BF16)]
                           * (2 * SAMPLE_HEAD_GROUPS),
        ),
        out_shape=jax.ShapeDtypeStruct((nb, n_new, a), F32),
        compiler_params=pltpu.CompilerParams(
            dimension_semantics=("arbitrary",), vmem_limit_bytes=_vmem_limit(vmem)),
        name="sample_diff_attention",
    )(page_table, *lams, sg, q, k_new, v_new, *([k_pool] * n_pages), *([v_pool] * n_pages))


def _pool_project(d, wp_ref, scale_ref, g, cols):
    return jnp.dot(d.astype(BF16), wp_ref[g], preferred_element_type=F32) * scale_ref[:, cols]


def _prompt_pool_tile(i, u_ref, halo_ref, wp_ref, scale_ref, z_ref, ext_ref):
    tm, c = u_ref.shape
    cg = c // len(POOL_WINDOWS)
    ext_ref[0:POOL_HALO, :] = jnp.where(i == 0, 0.0, halo_ref[...])
    ext_ref[POOL_HALO:, :] = u_ref[...]
    pos = i * tm + lax.broadcasted_iota(jnp.int32, (tm, 1), 0)
    for g, w in enumerate(POOL_WINDOWS):
        cols = slice(g * cg, (g + 1) * cg)
        cur = ext_ref[POOL_HALO:, cols]
        win = cur
        for s in range(1, w):
            win = win + ext_ref[POOL_HALO - s:POOL_HALO - s + tm, cols]
        cnt = jnp.minimum(w, pos + 1).astype(F32)
        d = win * (1.0 / cnt) - cur
        z_ref[:, cols] = _pool_project(d, wp_ref, scale_ref, g, cols).astype(z_ref.dtype)


def _sample_pool_body(state_ref, u_ref, wp_ref, scale_ref, z_ref, *, pos0):
    n_state = state_ref.shape[0]
    n_new, _, c = u_ref.shape
    cg = c // len(POOL_WINDOWS)

    def ext(r, cols):
        return state_ref[r, :, cols] if r < n_state else u_ref[r - n_state, :, cols]

    for g, w in enumerate(POOL_WINDOWS):
        cols = slice(g * cg, (g + 1) * cg)
        for t in range(n_new):
            cur = ext(n_state + t, cols)
            win = cur
            for s in range(1, w):
                win = win + ext(n_state + t - s, cols)
            d = win * (1.0 / min(w, pos0 + t + 1)) - cur
            z_ref[t, :, cols] = _pool_project(d, wp_ref, scale_ref, g, cols).astype(z_ref.dtype)


def _sample_pool(state_t, u_t, wp, scale, *, pos0):
    n_new, nb, c = u_t.shape
    vmem = 2 * (state_t.size + u_t.size) * 4 + wp.size * 4 + 4 * nb * c * 4
    return pl.pallas_call(
        functools.partial(_sample_pool_body, pos0=pos0),
        out_shape=jax.ShapeDtypeStruct((n_new, nb, c), BF16),
        compiler_params=pltpu.CompilerParams(vmem_limit_bytes=_vmem_limit(vmem)),
        name="sample_pool_mixer",
    )(state_t, u_t, wp, scale)


def _outproj_body(x_ref, o_ref, z_ref, w_ref, y_ref):
    a = o_ref.shape[1]
    y = x_ref[...] + jnp.dot(o_ref[...].astype(BF16), w_ref[0:a, :], preferred_element_type=F32)
    y_ref[...] = y + jnp.dot(z_ref[...].astype(BF16), w_ref[a:, :], preferred_element_type=F32)


def _outproj(x, o, z, w_out):
    m, d = x.shape
    a, c = o.shape[1], z.shape[1]
    tm = min(PROJ_TOKEN_TILE, m)
    assert m % tm == 0
    row = lambda i: (i, 0)
    vmem = 2 * 2 * tm * d * 4 + 2 * tm * (a + c) * 4 + w_out.size * 2 + tm * d * 4
    return pl.pallas_call(
        _outproj_body,
        grid=(m // tm,),
        in_specs=[
            pl.BlockSpec((tm, d), row), pl.BlockSpec((tm, a), row), pl.BlockSpec((tm, c), row),
            pl.BlockSpec(w_out.shape, lambda i: (0, 0), pipeline_mode=pl.Buffered(1)),
        ],
        out_specs=pl.BlockSpec((tm, d), row),
        out_shape=jax.ShapeDtypeStruct((m, d), F32),
        compiler_params=pltpu.CompilerParams(
            dimension_semantics=("parallel",), vmem_limit_bytes=_vmem_limit(vmem)),
        name="output_projection",
    )(x, o, z, w_out)


def _prompt_outproj_body(x_ref, o_ref, u_ref, halo_ref, wp_ref, scale_ref, w_ref, y_ref,
                         ext_ref, z_ref):
    a = o_ref.shape[1]
    y = x_ref[...] + jnp.dot(o_ref[...], w_ref[0:a, :], preferred_element_type=F32)
    _prompt_pool_tile(pl.program_id(1), u_ref, halo_ref, wp_ref, scale_ref, z_ref, ext_ref)
    y_ref[...] = y + jnp.dot(z_ref[...], w_ref[a:, :], preferred_element_type=F32)


def _prompt_outproj(x, o, u, wp, scale, w_out, *, batch, seq):
    m, d = x.shape
    a, c = o.shape[1], u.shape[1]
    tm = POOL_TOKEN_TILE
    nt = seq // tm
    halo_per_tile = tm // POOL_HALO
    assert seq % tm == 0 and o.dtype == BF16
    row = lambda b, i: (b * nt + i, 0)
    const = lambda b, i: (0, 0)
    vmem = (2 * 2 * tm * d * 4 + 2 * tm * (a * 2 + c * 4) + w_out.size * 2 + wp.size * 2 * 2
            + (tm + POOL_HALO) * c * 4 + tm * c * 2 + tm * d * 4 + 4 * tm * c * 4)
    return pl.pallas_call(
        _prompt_outproj_body,
        grid=(batch, nt),
        in_specs=[
            pl.BlockSpec((tm, d), row), pl.BlockSpec((tm, a), row), pl.BlockSpec((tm, c), row),
            pl.BlockSpec((POOL_HALO, c),
                         lambda b, i: (jnp.maximum((b * nt + i) * halo_per_tile - 1, 0), 0)),
            pl.BlockSpec(wp.shape, lambda b, i: (0, 0, 0)),
            pl.BlockSpec((1, c), const),
            pl.BlockSpec(w_out.shape, const, pipeline_mode=pl.Buffered(1)),
        ],
        out_specs=pl.BlockSpec((tm, d), row),
        out_shape=jax.ShapeDtypeStruct((m, d), F32),
        scratch_shapes=[pltpu.VMEM((tm + POOL_HALO, c), F32), pltpu.VMEM((tm, c), BF16)],
        compiler_params=pltpu.CompilerParams(
            dimension_semantics=("parallel", "arbitrary"), vmem_limit_bytes=_vmem_limit(vmem)),
        name="prompt_pool_output_projection",
    )(x, o, u, u, wp, scale, w_out)


def kernel(x_prompt, x_sample, cache_k, cache_v, state_pool, page_table, ffn1_norm, ffn1_w_gate, ffn1_w_up, ffn1_w_down, mix_norm, w_in, lambda_q1, lambda_k1, lambda_q2, lambda_k2, subln_gain, w_pool, pool_scale, w_out, ffn2_norm, ffn2_w_gate, ffn2_w_up, ffn2_w_down, final_norm):
    batch, seq, d = x_prompt.shape
    nb, n_new, _ = x_sample.shape
    depth, n_phys, page, n_heads, vd = cache_k.shape
    assert n_heads == N_HEADS
    a = n_heads * vd
    c = state_pool.shape[-1]
    past_len = page_table.shape[1] * page
    k_pool = cache_k.reshape(depth * n_phys * page * n_heads, vd)
    v_pool = cache_v.reshape(depth * n_phys * page * n_heads, vd)

    xp = x_prompt.reshape(batch * seq, d)
    xs = x_sample.reshape(nb * n_new, d)
    fg = final_norm.reshape(1, d)
    k_p, v_p, buf_p, k_s, v_s, buf_s = [], [], [], [], [], []
    for l in range(depth):
        lam_init = _lambda_init(l)
        last = l == depth - 1
        mix_g = mix_norm[l].reshape(1, d)
        w_out_b = w_out[l].astype(BF16)
        wp_b = w_pool[l].astype(BF16)
        scale = pool_scale[l].reshape(1, c)
        lams = tuple(p[l].reshape(1, -1) for p in (lambda_q1, lambda_k1, lambda_q2, lambda_k2))
        sg = subln_gain[l].reshape(1, vd)

        xp, xs = _ffn_both(xp, xs, ffn1_norm[l].reshape(1, d), ffn1_w_gate, ffn1_w_up, ffn1_w_down,
                           fg, layer=l, apply_final_norm=False)

        (qs, ks, vs, us, _, _), w_in_b = _inproj_converting(xs, mix_g, w_in, a, layer=l)

        qb, kp, vp, up, kb, vb = _inproj(xp, mix_g, w_in_b, a, layer=0)
        op = _prompt_attention(lams, sg, qb, kb, vb, batch=batch, seq=seq, lam_init=lam_init)
        xp = _prompt_outproj(xp, op, up, wp_b, scale, w_out_b, batch=batch, seq=seq)

        os_ = _sample_attention(
            page_table, lams, sg, qs.reshape(nb, n_new, a), ks.reshape(nb, n_new, a),
            vs.reshape(nb, n_new, a), k_pool, v_pool, page=page,
            first_page=l * n_phys, lam_init=lam_init)
        us3 = us.reshape(nb, n_new, c)
        zs_t = _sample_pool(jnp.transpose(state_pool[l], (1, 0, 2)), jnp.transpose(us3, (1, 0, 2)),
                            wp_b, scale, pos0=past_len)
        zs = jnp.transpose(zs_t, (1, 0, 2)).reshape(nb * n_new, c)
        xs = _outproj(xs, os_.reshape(nb * n_new, a), zs, w_out_b)

        xp, xs = _ffn_both(xp, xs, ffn2_norm[l].reshape(1, d), ffn2_w_gate, ffn2_w_up, ffn2_w_down,
                           fg, layer=l, apply_final_norm=last)

        k_p.append(kp.reshape(batch, seq, n_heads, vd))
        v_p.append(vp.reshape(batch, seq, n_heads, vd))
        buf_p.append(up.reshape(batch, seq, c)[:, seq - POOL_BUF:])
        k_s.append(ks.reshape(nb, n_new, n_heads, vd))
        v_s.append(vs.reshape(nb, n_new, n_heads, vd))
        buf_s.append(jnp.concatenate([state_pool[l], us3], axis=1)[:, -POOL_BUF:])

    return (xp.reshape(batch, seq, d), xs.reshape(nb, n_new, d),
            jnp.stack(k_p), jnp.stack(v_p), jnp.stack(buf_p),
            jnp.stack(k_s), jnp.stack(v_s), jnp.stack(buf_s))
```

```python
import functools
import math

import jax
import jax.numpy as jnp
from jax import lax
from jax.experimental import pallas as pl
from jax.experimental.pallas import tpu as pltpu

F32 = jnp.float32
BF16 = jnp.bfloat16

N_HEADS = 8
POOL_WINDOWS = (2, 4, 8, 16)
POOL_BUF = max(POOL_WINDOWS) - 1
FFN_RESIDUAL = 0.5
NORM_EPS = 1e-6
SUBLN_EPS = 1e-5
MASK_VALUE = -1e30

V7X_VMEM_BYTES = 64 * 1024 * 1024
LANES = 128

FFN_TOKEN_TILE = 512
FFN_HIDDEN_TILE = 1024
FFN_EMIT_HIDDEN_TILE = 256
PROJ_TOKEN_TILE = 512
ATTN_Q_TILE = 256
POOL_TOKEN_TILE = 512
POOL_HALO = 16
SAMPLE_HEAD_GROUPS = 2
SAMPLE_KEY_PAD = 32


def _vmem_limit(nbytes):
    return int(min(nbytes * 5 // 4 + (8 << 20), V7X_VMEM_BYTES - (4 << 20)))


def _rmsnorm(x, g, eps):
    return x * lax.rsqrt(jnp.mean(x * x, axis=-1, keepdims=True) + eps) * g


def _lambda_init(layer):
    return 0.8 - 0.6 * math.exp(-0.3 * layer)


def _diff_lambda(lq1_ref, lk1_ref, lq2_ref, lk2_ref, lam_init):
    a = jnp.sum(lq1_ref[...] * lk1_ref[...], axis=-1, keepdims=True)
    b = jnp.sum(lq2_ref[...] * lk2_ref[...], axis=-1, keepdims=True)
    return jnp.exp(a) - jnp.exp(b) + lam_init


def _swiglu_partial(h, wg, wu, wd):
    gate = jnp.dot(h, wg, preferred_element_type=F32)
    up = jnp.dot(h, wu, preferred_element_type=F32)
    act = (FFN_RESIDUAL * (gate * jax.nn.sigmoid(gate)) * up).astype(BF16)
    return jnp.dot(act, wd, preferred_element_type=F32)


def _ffn_body(x_ref, g_ref, wg_ref, wu_ref, wd_ref, wg_rem_ref, wu_rem_ref, wd_rem_ref, fg_ref,
              o_ref, *rest, apply_final_norm, emit_bf16_weights):
    f = pl.program_id(1)
    h_ref = rest[-1]

    @pl.when(f == 0)
    def _():
        x = x_ref[...]
        h = _rmsnorm(x, g_ref[...], NORM_EPS).astype(BF16)
        h_ref[...] = h
        o_ref[...] = x + _swiglu_partial(h, wg_rem_ref[...], wu_rem_ref[...], wd_rem_ref[...])

    if emit_bf16_weights:
        wg, wu, wd = (r[...].astype(BF16) for r in (wg_ref, wu_ref, wd_ref))
        for out_ref, w in zip(rest[:3], (wg, wu, wd)):
            out_ref[...] = w
    else:
        wg, wu, wd = wg_ref[...], wu_ref[...], wd_ref[...]
    o_ref[...] += _swiglu_partial(h_ref[...], wg, wu, wd)

    if apply_final_norm:
        @pl.when(f == pl.num_programs(1) - 1)
        def _():
            o_ref[...] = _rmsnorm(o_ref[...], fg_ref[...], NORM_EPS)


def _ffn(x, g, wg, wu, wd, w_rem, fg, *, layer, n_main, apply_final_norm,
         emit_bf16_weights=False):
    m, d = x.shape
    wg_rem, wu_rem, wd_rem = w_rem
    rem = wg_rem.shape[1]
    tm = min(FFN_TOKEN_TILE, m)
    tf = FFN_EMIT_HIDDEN_TILE if emit_bf16_weights else FFN_HIDDEN_TILE
    assert m % tm == 0 and n_main % tf == 0 and rem % LANES == 0
    assert wg.dtype == (F32 if emit_bf16_weights else BF16)
    assert not emit_bf16_weights or m == tm
    const = lambda i, f: (0, 0)
    resident = functools.partial(pl.BlockSpec, index_map=const, pipeline_mode=pl.Buffered(1))
    col_tile = pl.BlockSpec((None, d, tf), lambda i, f: (layer, 0, f))
    row_tile = pl.BlockSpec((None, tf, d), lambda i, f: (layer, f, 0))
    tok_tile = pl.BlockSpec((tm, d), lambda i, f: (i, 0))
    out_specs, out_shape = [tok_tile], [jax.ShapeDtypeStruct((m, d), F32)]
    if emit_bf16_weights:
        out_specs += [pl.BlockSpec((None, d, tf), lambda i, f: (0, 0, f))] * 2
        out_specs += [pl.BlockSpec((None, tf, d), lambda i, f: (0, f, 0))]
        out_shape += [jax.ShapeDtypeStruct((1, d, n_main), BF16)] * 2
        out_shape += [jax.ShapeDtypeStruct((1, n_main, d), BF16)]
    w_bytes = wg.dtype.itemsize + (2 if emit_bf16_weights else 0)
    vmem = (2 * 2 * tm * d * 4 + tm * d * 2 + 2 * 3 * d * tf * w_bytes + 3 * d * rem * 2
            + 3 * tm * tf * 4)
    outs = pl.pallas_call(
        functools.partial(_ffn_body, apply_final_norm=apply_final_norm,
                          emit_bf16_weights=emit_bf16_weights),
        grid=(m // tm, n_main // tf),
        in_specs=[
            tok_tile,
            pl.BlockSpec((1, d), const),
            col_tile, col_tile, row_tile,
            resident((d, rem)), resident((d, rem)), resident((rem, d)),
            pl.BlockSpec((1, d), const),
        ],
        out_specs=out_specs,
        out_shape=out_shape,
        scratch_shapes=[pltpu.VMEM((tm, d), BF16)],
        compiler_params=pltpu.CompilerParams(
            dimension_semantics=("parallel", "arbitrary"),
            vmem_limit_bytes=_vmem_limit(vmem)),
        name="ffn_half_step",
    )(x, g, wg, wu, wd, wg_rem, wu_rem, wd_rem, fg)
    return outs if emit_bf16_weights else outs[0]


def _ffn_both(xp, xs, g, wg, wu, wd, fg, *, layer, apply_final_norm):
    d_ff = wg.shape[-1]
    n_main = d_ff // FFN_HIDDEN_TILE * FFN_HIDDEN_TILE
    w_rem = (wg[layer, :, n_main:].astype(BF16), wu[layer, :, n_main:].astype(BF16),
             wd[layer, n_main:, :].astype(BF16))
    xs, wg_b, wu_b, wd_b = _ffn(xs, g, wg, wu, wd, w_rem, fg, layer=layer, n_main=n_main,
                                apply_final_norm=apply_final_norm, emit_bf16_weights=True)
    xp = _ffn(xp, g, wg_b, wu_b, wd_b, w_rem, fg, layer=0, n_main=n_main,
              apply_final_norm=apply_final_norm)
    return xp, xs


def _inproj_body(x_ref, g_ref, w_ref, q_ref, k_ref, v_ref, u_ref, kb_ref, vb_ref, *, q_scale):
    h = _rmsnorm(x_ref[...], g_ref[...], NORM_EPS).astype(BF16)
    a = q_ref.shape[1]
    q_ref[...] = (jnp.dot(h, w_ref[:, 0:a], preferred_element_type=F32) * q_scale).astype(BF16)
    k = jnp.dot(h, w_ref[:, a:2 * a], preferred_element_type=F32)
    k_ref[...] = k
    kb_ref[...] = k.astype(BF16)
    v = jnp.dot(h, w_ref[:, 2 * a:3 * a], preferred_element_type=F32)
    v_ref[...] = v
    vb_ref[...] = v.astype(BF16)
    u_ref[...] = jnp.dot(h, w_ref[:, 3 * a:], preferred_element_type=F32)


def _inproj_outputs(m, a, pw):
    return [jax.ShapeDtypeStruct((m, a), BF16), jax.ShapeDtypeStruct((m, a), F32),
            jax.ShapeDtypeStruct((m, a), F32), jax.ShapeDtypeStruct((m, pw), F32),
            jax.ShapeDtypeStruct((m, a), BF16), jax.ShapeDtypeStruct((m, a), BF16)]


def _inproj_converting_body(x_ref, g_ref, w_ref, q_ref, k_ref, v_ref, u_ref, kb_ref, vb_ref,
                            wb_ref, h_ref, *, q_scale):
    j = pl.program_id(0)

    @pl.when(j == 0)
    def _():
        h_ref[...] = _rmsnorm(x_ref[...], g_ref[...], NORM_EPS).astype(BF16)

    w = w_ref[...].astype(BF16)
    wb_ref[...] = w
    y = jnp.dot(h_ref[...], w, preferred_element_type=F32)

    @pl.when(j == 0)
    def _():
        q_ref[...] = (y * q_scale).astype(BF16)

    @pl.when(j == 1)
    def _():
        k_ref[...] = y
        kb_ref[...] = y.astype(BF16)

    @pl.when(j == 2)
    def _():
        v_ref[...] = y
        vb_ref[...] = y.astype(BF16)

    @pl.when(j == 3)
    def _():
        u_ref[...] = y


def _inproj_converting(x, g, w_in, attn_width, *, layer):
    m, d = x.shape
    n = w_in.shape[-1]
    a = attn_width
    head_dim = a // (2 * N_HEADS)
    assert n == 4 * a and m <= PROJ_TOKEN_TILE
    whole = lambda j: (0, 0)
    vmem = (m * d * 4 + 2 * d * a * (4 + 2) + 2 * m * a * (2 + 4 + 4 + 4 + 2 + 2) + m * d * 2
            + m * a * 4)
    *outs, w_bf16 = pl.pallas_call(
        functools.partial(_inproj_converting_body, q_scale=head_dim ** -0.5),
        grid=(4,),
        in_specs=[
            pl.BlockSpec((m, d), whole, pipeline_mode=pl.Buffered(1)),
            pl.BlockSpec((1, d), whole),
            pl.BlockSpec((None, d, a), lambda j: (layer, 0, j)),
        ],
        out_specs=[pl.BlockSpec((m, a), whole)] * 6
                  + [pl.BlockSpec((None, d, a), lambda j: (0, 0, j))],
        out_shape=_inproj_outputs(m, a, a) + [jax.ShapeDtypeStruct((1, d, n), BF16)],
        scratch_shapes=[pltpu.VMEM((m, d), BF16)],
        compiler_params=pltpu.CompilerParams(
            dimension_semantics=("arbitrary",), vmem_limit_bytes=_vmem_limit(vmem)),
        name="input_projection_converting",
    )(x, g, w_in)
    return outs, w_bf16


def _inproj(x, g, w_in, attn_width, *, layer):
    m, d = x.shape
    n = w_in.shape[-1]
    a = attn_width
    pw = n - 3 * a
    head_dim = a // (2 * N_HEADS)
    tm = min(PROJ_TOKEN_TILE, m)
    assert m % tm == 0
    vmem = 2 * tm * d * 4 + d * n * 2 + 2 * tm * (a * 2 + 2 * a * 6 + pw * 4) + tm * n * 4
    row = lambda i: (i, 0)
    return pl.pallas_call(
        functools.partial(_inproj_body, q_scale=head_dim ** -0.5),
        grid=(m // tm,),
        in_specs=[
            pl.BlockSpec((tm, d), row),
            pl.BlockSpec((1, d), lambda i: (0, 0)),
            pl.BlockSpec((None, d, n), lambda i: (layer, 0, 0), pipeline_mode=pl.Buffered(1)),
        ],
        out_specs=[
            pl.BlockSpec((tm, a), row), pl.BlockSpec((tm, a), row), pl.BlockSpec((tm, a), row),
            pl.BlockSpec((tm, pw), row), pl.BlockSpec((tm, a), row), pl.BlockSpec((tm, a), row),
        ],
        out_shape=_inproj_outputs(m, a, pw),
        compiler_params=pltpu.CompilerParams(
            dimension_semantics=("parallel",), vmem_limit_bytes=_vmem_limit(vmem)),
        name="input_projection",
    )(x, g, w_in)


def _subln(o, gain, lam_init):
    return _rmsnorm(o, gain, SUBLN_EPS) * (1.0 - lam_init)


def _prompt_attn_body(lq1_ref, lk1_ref, lq2_ref, lk2_ref, sg_ref, q_ref, k_ref, v_ref, o_ref, *,
                      lam_init):
    seq, vd = q_ref.shape
    hd = vd // 2
    tq = ATTN_Q_TILE
    nt = (((1,), (1,)), ((), ()))
    lam = _diff_lambda(lq1_ref, lk1_ref, lq2_ref, lk2_ref, lam_init)
    lane = lax.broadcasted_iota(jnp.int32, (tq, vd), 1)
    row = lax.broadcasted_iota(jnp.int32, (2 * tq, tq), 0)
    row = jnp.where(row >= tq, row - tq, row)
    col = lax.broadcasted_iota(jnp.int32, (2 * tq, tq), 1)
    causal = col <= row

    for i in range(seq // tq):
        lo, hi = i * tq, (i + 1) * tq
        q = q_ref[lo:hi, :]
        zero = jnp.zeros_like(q)
        qz = jnp.concatenate([jnp.where(lane < hd, q, zero), jnp.where(lane >= hd, q, zero)],
                             axis=0)
        s_diag = lax.dot_general(qz, k_ref[lo:hi, :], nt, preferred_element_type=F32)
        s_diag = jnp.where(causal, s_diag, MASK_VALUE)
        m = jnp.max(s_diag, axis=-1, keepdims=True)
        if i > 0:
            s_past = lax.dot_general(qz, k_ref[0:lo, :], nt, preferred_element_type=F32)
            m = jnp.maximum(m, jnp.max(s_past, axis=-1, keepdims=True))
        p_diag = jnp.exp(s_diag - m)
        l = jnp.sum(p_diag, axis=-1, keepdims=True)
        acc = jnp.dot(p_diag.astype(BF16), v_ref[lo:hi, :], preferred_element_type=F32)
        if i > 0:
            p_past = jnp.exp(s_past - m)
            l = l + jnp.sum(p_past, axis=-1, keepdims=True)
            acc = acc + jnp.dot(p_past.astype(BF16), v_ref[0:lo, :], preferred_element_type=F32)
        on = acc * (1.0 / l)
        o = on[:tq] - lam * on[tq:]
        o_ref[lo:hi, :] = _subln(o, sg_ref[...], lam_init).astype(o_ref.dtype)


def _prompt_attention(lams, sg, qb, kb, vb, *, batch, seq, lam_init):
    m, a = qb.shape
    vd = a // N_HEADS
    tq = ATTN_Q_TILE
    assert seq % tq == 0
    lam_spec = pl.BlockSpec((1, vd // 2), lambda b, h: (0, 0))
    head_spec = pl.BlockSpec((seq, vd), lambda b, h: (b, h))
    vmem = 2 * 4 * seq * vd * 2 + 4 * 2 * tq * seq * 4
    return pl.pallas_call(
        functools.partial(_prompt_attn_body, lam_init=lam_init),
        grid=(batch, N_HEADS),
        in_specs=[lam_spec] * 4 + [pl.BlockSpec((1, vd), lambda b, h: (0, 0)),
                                   head_spec, head_spec, head_spec],
        out_specs=head_spec,
        out_shape=jax.ShapeDtypeStruct((m, a), BF16),
        compiler_params=pltpu.CompilerParams(
            dimension_semantics=("parallel", "parallel"),
            vmem_limit_bytes=_vmem_limit(vmem)),
        name="prompt_diff_attention",
    )(*lams, sg, qb, kb, vb)


def _sample_attn_body(pt_ref, lq1_ref, lk1_ref, lq2_ref, lk2_ref, sg_ref, qkv_ref, *rest, n_pages,
                      lam_init):
    del pt_ref
    k_pages = rest[:n_pages]
    v_pages = rest[n_pages:2 * n_pages]
    o_ref = rest[2 * n_pages]
    n_groups = SAMPLE_HEAD_GROUPS
    kbf_refs = rest[2 * n_pages + 1:2 * n_pages + 1 + n_groups]
    vbf_refs = rest[2 * n_pages + 1 + n_groups:]
    hg = N_HEADS // n_groups
    page_rows = k_pages[0].shape[0] // n_groups
    past_rows = n_pages * page_rows
    n_rows_k = kbf_refs[0].shape[0]
    _, n_new, a = qkv_ref.shape
    vd = a // N_HEADS
    hd = vd // 2

    lam = _diff_lambda(lq1_ref, lk1_ref, lq2_ref, lk2_ref, lam_init)
    qf, kn, vn = qkv_ref[0], qkv_ref[1], qkv_ref[2]
    n_q = 2 * hg * n_new
    r = lax.broadcasted_iota(jnp.int32, (n_q, vd), 0)
    c = lax.broadcasted_iota(jnp.int32, (n_q, vd), 1)
    q_half_mask = (c // hd) == (r // (hg * n_new))
    col = lax.broadcasted_iota(jnp.int32, (n_q, n_rows_k), 1)
    row = lax.broadcasted_iota(jnp.int32, (n_q, n_rows_k), 0)
    q_head, q_tok = (row // n_new) % hg, row % n_new
    new_col = col - past_rows
    visible = (((col < past_rows) & ((col % hg) == q_head))
               | ((new_col >= 0) & (new_col < hg * n_new) & ((new_col // n_new) == q_head)
                  & ((new_col % n_new) <= q_tok)))
    pad = jnp.zeros((n_rows_k - past_rows - hg * n_new, vd), F32)

    for g in range(n_groups):
        head_cols = [slice(h * vd, (h + 1) * vd) for h in range(g, N_HEADS, n_groups)]
        kbf_ref, vbf_ref = kbf_refs[g], vbf_refs[g]
        group_rows = pl.ds(g, page_rows, stride=n_groups)
        for j in range(n_pages):
            rows = slice(j * page_rows, (j + 1) * page_rows)
            kbf_ref[rows, :] = k_pages[j][group_rows, :].astype(BF16)
            vbf_ref[rows, :] = v_pages[j][group_rows, :].astype(BF16)
        kbf_ref[past_rows:, :] = jnp.concatenate([kn[:, hc] for hc in head_cols] + [pad],
                                                 axis=0).astype(BF16)
        vbf_ref[past_rows:, :] = jnp.concatenate([vn[:, hc] for hc in head_cols] + [pad],
                                                 axis=0).astype(BF16)

        q2 = jnp.concatenate([qf[:, hc] for hc in head_cols] * 2, axis=0)
        q2 = jnp.where(q_half_mask, q2, 0.0).astype(BF16)
        s = lax.dot_general(q2, kbf_ref[...], (((1,), (1,)), ((), ())),
                            preferred_element_type=F32)
        s = jnp.where(visible, s, MASK_VALUE)
        p = jnp.exp(s - jnp.max(s, axis=-1, keepdims=True))
        p = p * (1.0 / jnp.sum(p, axis=-1, keepdims=True))
        attn = (p[:n_q // 2] - lam * p[n_q // 2:]).astype(BF16)
        o = jnp.dot(attn, vbf_ref[...], preferred_element_type=F32)
        o = _subln(o, sg_ref[...], lam_init)
        for i, hc in enumerate(head_cols):
            o_ref[:, hc] = o[i * n_new:(i + 1) * n_new]


def _sample_attention(page_table, lams, sg, qkv_new, k_pool, v_pool, *, page, first_page,
                      lam_init):
    nb, _, n_new, a = qkv_new.shape
    n_pages = page_table.shape[1]
    vd = a // N_HEADS
    page_rows = page * N_HEADS
    n_keys = n_pages * page + SAMPLE_KEY_PAD
    const = lambda b, pt: (0, 0)
    tok_spec = pl.BlockSpec((None, n_new, a), lambda b, pt: (b, 0, 0))
    qkv_spec = pl.BlockSpec((None, 3, n_new, a), lambda b, pt: (b, 0, 0, 0))
    page_specs = [pl.BlockSpec((page_rows, vd), lambda b, pt, j=j: (first_page + pt[b, j], 0))
                  for j in range(n_pages)]
    vmem = 2 * 2 * n_pages * page * a * 4 + 2 * n_keys * a * 2 + 8 * 16 * n_keys * 4
    return pl.pallas_call(
        functools.partial(_sample_attn_body, n_pages=n_pages, lam_init=lam_init),
        grid_spec=pltpu.PrefetchScalarGridSpec(
            num_scalar_prefetch=1,
            grid=(nb,),
            in_specs=[pl.BlockSpec((1, vd // 2), const)] * 4 + [pl.BlockSpec((1, vd), const)]
                     + [qkv_spec] + page_specs + page_specs,
            out_specs=tok_spec,
            scratch_shapes=[pltpu.VMEM((n_keys * N_HEADS // SAMPLE_HEAD_GROUPS, vd), BF16)]
                           * (2 * SAMPLE_HEAD_GROUPS),
        ),
        out_shape=jax.ShapeDtypeStruct((nb, n_new, a), F32),
        compiler_params=pltpu.CompilerParams(
            dimension_semantics=("arbitrary",), vmem_limit_bytes=_vmem_limit(vmem)),
        name="sample_diff_attention",
    )(page_table, *lams, sg, qkv_new, *([k_pool] * n_pages), *([v_pool] * n_pages))


def _pool_project(d, wp_ref, scale_ref, g, cols):
    return jnp.dot(d.astype(BF16), wp_ref[g], preferred_element_type=F32) * scale_ref[:, cols]


def _prompt_pool_tile(i, u_ref, halo_ref, wp_ref, scale_ref, z_ref, ext_ref):
    tm, c = u_ref.shape
    cg = c // len(POOL_WINDOWS)
    ext_ref[0:POOL_HALO, :] = jnp.where(i == 0, 0.0, halo_ref[...])
    ext_ref[POOL_HALO:, :] = u_ref[...]
    pos = i * tm + lax.broadcasted_iota(jnp.int32, (tm, 1), 0)
    for g, w in enumerate(POOL_WINDOWS):
        cols = slice(g * cg, (g + 1) * cg)
        cur = ext_ref[POOL_HALO:, cols]
        win = cur
        for s in range(1, w):
            win = win + ext_ref[POOL_HALO - s:POOL_HALO - s + tm, cols]
        cnt = jnp.minimum(w, pos + 1).astype(F32)
        d = win * (1.0 / cnt) - cur
        z_ref[:, cols] = _pool_project(d, wp_ref, scale_ref, g, cols).astype(z_ref.dtype)


def _sample_pool_body(state_ref, u_ref, wp_ref, scale_ref, z_ref, *, pos0):
    n_state = state_ref.shape[0]
    n_new, _, c = u_ref.shape
    cg = c // len(POOL_WINDOWS)

    def ext(r, cols):
        return state_ref[r, :, cols] if r < n_state else u_ref[r - n_state, :, cols]

    for g, w in enumerate(POOL_WINDOWS):
        cols = slice(g * cg, (g + 1) * cg)
        for t in range(n_new):
            cur = ext(n_state + t, cols)
            win = cur
            for s in range(1, w):
                win = win + ext(n_state + t - s, cols)
            d = win * (1.0 / min(w, pos0 + t + 1)) - cur
            z_ref[t, :, cols] = _pool_project(d, wp_ref, scale_ref, g, cols).astype(z_ref.dtype)


def _sample_pool(state_t, u_t, wp, scale, *, pos0):
    n_new, nb, c = u_t.shape
    vmem = 2 * (state_t.size + u_t.size) * 4 + wp.size * 4 + 4 * nb * c * 4
    return pl.pallas_call(
        functools.partial(_sample_pool_body, pos0=pos0),
        out_shape=jax.ShapeDtypeStruct((n_new, nb, c), BF16),
        compiler_params=pltpu.CompilerParams(vmem_limit_bytes=_vmem_limit(vmem)),
        name="sample_pool_mixer",
    )(state_t, u_t, wp, scale)


def _outproj_body(x_ref, o_ref, z_ref, w_ref, y_ref):
    a = o_ref.shape[1]
    y = x_ref[...] + jnp.dot(o_ref[...].astype(BF16), w_ref[0:a, :], preferred_element_type=F32)
    y_ref[...] = y + jnp.dot(z_ref[...].astype(BF16), w_ref[a:, :], preferred_element_type=F32)


def _outproj(x, o, z, w_out):
    m, d = x.shape
    a, c = o.shape[1], z.shape[1]
    tm = min(PROJ_TOKEN_TILE, m)
    assert m % tm == 0
    row = lambda i: (i, 0)
    vmem = 2 * 2 * tm * d * 4 + 2 * tm * (a + c) * 4 + w_out.size * 2 + tm * d * 4
    return pl.pallas_call(
        _outproj_body,
        grid=(m // tm,),
        in_specs=[
            pl.BlockSpec((tm, d), row), pl.BlockSpec((tm, a), row), pl.BlockSpec((tm, c), row),
            pl.BlockSpec(w_out.shape, lambda i: (0, 0), pipeline_mode=pl.Buffered(1)),
        ],
        out_specs=pl.BlockSpec((tm, d), row),
        out_shape=jax.ShapeDtypeStruct((m, d), F32),
        compiler_params=pltpu.CompilerParams(
            dimension_semantics=("parallel",), vmem_limit_bytes=_vmem_limit(vmem)),
        name="output_projection",
    )(x, o, z, w_out)


def _prompt_outproj_body(x_ref, o_ref, u_ref, halo_ref, wp_ref, scale_ref, w_ref, y_ref,
                         ext_ref, z_ref):
    a = o_ref.shape[1]
    y = x_ref[...] + jnp.dot(o_ref[...], w_ref[0:a, :], preferred_element_type=F32)
    _prompt_pool_tile(pl.program_id(1), u_ref, halo_ref, wp_ref, scale_ref, z_ref, ext_ref)
    y_ref[...] = y + jnp.dot(z_ref[...], w_ref[a:, :], preferred_element_type=F32)


def _prompt_outproj(x, o, u, wp, scale, w_out, *, batch, seq):
    m, d = x.shape
    a, c = o.shape[1], u.shape[1]
    tm = POOL_TOKEN_TILE
    nt = seq // tm
    halo_per_tile = tm // POOL_HALO
    assert seq % tm == 0 and o.dtype == BF16
    row = lambda b, i: (b * nt + i, 0)
    const = lambda b, i: (0, 0)
    vmem = (2 * 2 * tm * d * 4 + 2 * tm * (a * 2 + c * 4) + w_out.size * 2 + wp.size * 2 * 2
            + (tm + POOL_HALO) * c * 4 + tm * c * 2 + tm * d * 4 + 4 * tm * c * 4)
    return pl.pallas_call(
        _prompt_outproj_body,
        grid=(batch, nt),
        in_specs=[
            pl.BlockSpec((tm, d), row), pl.BlockSpec((tm, a), row), pl.BlockSpec((tm, c), row),
            pl.BlockSpec((POOL_HALO, c),
                         lambda b, i: (jnp.maximum((b * nt + i) * halo_per_tile - 1, 0), 0)),
            pl.BlockSpec(wp.shape, lambda b, i: (0, 0, 0)),
            pl.BlockSpec((1, c), const),
            pl.BlockSpec(w_out.shape, const, pipeline_mode=pl.Buffered(1)),
        ],
        out_specs=pl.BlockSpec((tm, d), row),
        out_shape=jax.ShapeDtypeStruct((m, d), F32),
        scratch_shapes=[pltpu.VMEM((tm + POOL_HALO, c), F32), pltpu.VMEM((tm, c), BF16)],
        compiler_params=pltpu.CompilerParams(
            dimension_semantics=("parallel", "arbitrary"), vmem_limit_bytes=_vmem_limit(vmem)),
        name="prompt_pool_output_projection",
    )(x, o, u, u, wp, scale, w_out)


def kernel(x_prompt, x_sample, cache_k, cache_v, state_pool, page_table, ffn1_norm, ffn1_w_gate, ffn1_w_up, ffn1_w_down, mix_norm, w_in, lambda_q1, lambda_k1, lambda_q2, lambda_k2, subln_gain, w_pool, pool_scale, w_out, ffn2_norm, ffn2_w_gate, ffn2_w_up, ffn2_w_down, final_norm):
    batch, seq, d = x_prompt.shape
    nb, n_new, _ = x_sample.shape
    depth, n_phys, page, n_heads, vd = cache_k.shape
    assert n_heads == N_HEADS
    a = n_heads * vd
    c = state_pool.shape[-1]
    past_len = page_table.shape[1] * page
    k_pool = cache_k.reshape(depth * n_phys * page * n_heads, vd)
    v_pool = cache_v.reshape(depth * n_phys * page * n_heads, vd)

    xp = x_prompt.reshape(batch * seq, d)
    xs = x_sample.reshape(nb * n_new, d)
    fg = final_norm.reshape(1, d)
    k_p, v_p, buf_p, k_s, v_s, buf_s = [], [], [], [], [], []
    for l in range(depth):
        lam_init = _lambda_init(l)
        last = l == depth - 1
        mix_g = mix_norm[l].reshape(1, d)
        w_out_b = w_out[l].astype(BF16)
        wp_b = w_pool[l].astype(BF16)
        scale = pool_scale[l].reshape(1, c)
        lams = tuple(p[l].reshape(1, -1) for p in (lambda_q1, lambda_k1, lambda_q2, lambda_k2))
        sg = subln_gain[l].reshape(1, vd)

        xp, xs = _ffn_both(xp, xs, ffn1_norm[l].reshape(1, d), ffn1_w_gate, ffn1_w_up, ffn1_w_down,
                           fg, layer=l, apply_final_norm=False)

        (qs, ks, vs, us, _, _), w_in_b = _inproj_converting(xs, mix_g, w_in, a, layer=l)

        qb, kp, vp, up, kb, vb = _inproj(xp, mix_g, w_in_b, a, layer=0)
        op = _prompt_attention(lams, sg, qb, kb, vb, batch=batch, seq=seq, lam_init=lam_init)
        xp = _prompt_outproj(xp, op, up, wp_b, scale, w_out_b, batch=batch, seq=seq)

        qkv_new = jnp.stack([t.astype(F32).reshape(nb, n_new, a) for t in (qs, ks, vs)], axis=1)
        os_ = _sample_attention(page_table, lams, sg, qkv_new, k_pool, v_pool, page=page,
                                first_page=l * n_phys, lam_init=lam_init)
        us3 = us.reshape(nb, n_new, c)
        zs_t = _sample_pool(jnp.transpose(state_pool[l], (1, 0, 2)), jnp.transpose(us3, (1, 0, 2)),
                            wp_b, scale, pos0=past_len)
        zs = jnp.transpose(zs_t, (1, 0, 2)).reshape(nb * n_new, c)
        xs = _outproj(xs, os_.reshape(nb * n_new, a), zs, w_out_b)

        xp, xs = _ffn_both(xp, xs, ffn2_norm[l].reshape(1, d), ffn2_w_gate, ffn2_w_up, ffn2_w_down,
                           fg, layer=l, apply_final_norm=last)

        k_p.append(kp.reshape(batch, seq, n_heads, vd))
        v_p.append(vp.reshape(batch, seq, n_heads, vd))
        buf_p.append(up.reshape(batch, seq, c)[:, seq - POOL_BUF:])
        k_s.append(ks.reshape(nb, n_new, n_heads, vd))
        v_s.append(vs.reshape(nb, n_new, n_heads, vd))
        buf_s.append(jnp.concatenate([state_pool[l], us3], axis=1)[:, -POOL_BUF:])

    return (xp.reshape(batch, seq, d), xs.reshape(nb, n_new, d),
            jnp.stack(k_p), jnp.stack(v_p), jnp.stack(buf_p),
            jnp.stack(k_s), jnp.stack(v_s), jnp.stack(buf_s))
```

```python
import functools
import math

import jax
import jax.numpy as jnp
from jax import lax
from jax.experimental import pallas as pl
from jax.experimental.pallas import tpu as pltpu

F32 = jnp.float32
BF16 = jnp.bfloat16

N_HEADS = 8
POOL_WINDOWS = (2, 4, 8, 16)
POOL_BUF = max(POOL_WINDOWS) - 1
FFN_RESIDUAL = 0.5
NORM_EPS = 1e-6
SUBLN_EPS = 1e-5
MASK_VALUE = -1e30

V7X_VMEM_BYTES = 64 * 1024 * 1024
LANES = 128

FFN_TOKEN_TILE = 512
FFN_HIDDEN_TILE = 1024
FFN_EMIT_HIDDEN_TILE = 256
PROJ_TOKEN_TILE = 512
ATTN_Q_TILE = 256
POOL_TOKEN_TILE = 512
POOL_HALO = 16
SAMPLE_HEAD_GROUPS = 2
SAMPLE_CHUNK_PAGES = 4
SAMPLE_KEY_PAD = 32


def _vmem_limit(nbytes):
    return int(min(nbytes * 5 // 4 + (8 << 20), V7X_VMEM_BYTES - (4 << 20)))


def _rmsnorm(x, g, eps):
    return x * lax.rsqrt(jnp.mean(x * x, axis=-1, keepdims=True) + eps) * g


def _lambda_init(layer):
    return 0.8 - 0.6 * math.exp(-0.3 * layer)


def _diff_lambda(lq1_ref, lk1_ref, lq2_ref, lk2_ref, lam_init):
    a = jnp.sum(lq1_ref[...] * lk1_ref[...], axis=-1, keepdims=True)
    b = jnp.sum(lq2_ref[...] * lk2_ref[...], axis=-1, keepdims=True)
    return jnp.exp(a) - jnp.exp(b) + lam_init


def _swiglu_partial(h, wg, wu, wd):
    gate = jnp.dot(h, wg, preferred_element_type=F32)
    up = jnp.dot(h, wu, preferred_element_type=F32)
    act = (FFN_RESIDUAL * (gate * jax.nn.sigmoid(gate)) * up).astype(BF16)
    return jnp.dot(act, wd, preferred_element_type=F32)


def _ffn_body(x_ref, g_ref, wg_ref, wu_ref, wd_ref, wg_rem_ref, wu_rem_ref, wd_rem_ref, fg_ref,
              o_ref, *rest, apply_final_norm, emit_bf16_weights):
    f = pl.program_id(1)
    h_ref = rest[-1]

    @pl.when(f == 0)
    def _():
        x = x_ref[...]
        h = _rmsnorm(x, g_ref[...], NORM_EPS).astype(BF16)
        h_ref[...] = h
        o_ref[...] = x + _swiglu_partial(h, wg_rem_ref[...], wu_rem_ref[...], wd_rem_ref[...])

    if emit_bf16_weights:
        wg, wu, wd = (r[...].astype(BF16) for r in (wg_ref, wu_ref, wd_ref))
        for out_ref, w in zip(rest[:3], (wg, wu, wd)):
            out_ref[...] = w
    else:
        wg, wu, wd = wg_ref[...], wu_ref[...], wd_ref[...]
    o_ref[...] += _swiglu_partial(h_ref[...], wg, wu, wd)

    if apply_final_norm:
        @pl.when(f == pl.num_programs(1) - 1)
        def _():
            o_ref[...] = _rmsnorm(o_ref[...], fg_ref[...], NORM_EPS)


def _ffn(x, g, wg, wu, wd, w_rem, fg, *, layer, n_main, apply_final_norm,
         emit_bf16_weights=False):
    m, d = x.shape
    wg_rem, wu_rem, wd_rem = w_rem
    rem = wg_rem.shape[1]
    tm = min(FFN_TOKEN_TILE, m)
    tf = FFN_EMIT_HIDDEN_TILE if emit_bf16_weights else FFN_HIDDEN_TILE
    assert m % tm == 0 and n_main % tf == 0 and rem % LANES == 0
    assert wg.dtype == (F32 if emit_bf16_weights else BF16)
    assert not emit_bf16_weights or m == tm
    const = lambda i, f: (0, 0)
    resident = functools.partial(pl.BlockSpec, index_map=const, pipeline_mode=pl.Buffered(1))
    col_tile = pl.BlockSpec((None, d, tf), lambda i, f: (layer, 0, f))
    row_tile = pl.BlockSpec((None, tf, d), lambda i, f: (layer, f, 0))
    tok_tile = pl.BlockSpec((tm, d), lambda i, f: (i, 0))
    out_specs, out_shape = [tok_tile], [jax.ShapeDtypeStruct((m, d), F32)]
    if emit_bf16_weights:
        out_specs += [pl.BlockSpec((None, d, tf), lambda i, f: (0, 0, f))] * 2
        out_specs += [pl.BlockSpec((None, tf, d), lambda i, f: (0, f, 0))]
        out_shape += [jax.ShapeDtypeStruct((1, d, n_main), BF16)] * 2
        out_shape += [jax.ShapeDtypeStruct((1, n_main, d), BF16)]
    w_bytes = wg.dtype.itemsize + (2 if emit_bf16_weights else 0)
    vmem = (2 * 2 * tm * d * 4 + tm * d * 2 + 2 * 3 * d * tf * w_bytes + 3 * d * rem * 2
            + 3 * tm * tf * 4)
    outs = pl.pallas_call(
        functools.partial(_ffn_body, apply_final_norm=apply_final_norm,
                          emit_bf16_weights=emit_bf16_weights),
        grid=(m // tm, n_main // tf),
        in_specs=[
            tok_tile,
            pl.BlockSpec((1, d), const),
            col_tile, col_tile, row_tile,
            resident((d, rem)), resident((d, rem)), resident((rem, d)),
            pl.BlockSpec((1, d), const),
        ],
        out_specs=out_specs,
        out_shape=out_shape,
        scratch_shapes=[pltpu.VMEM((tm, d), BF16)],
        compiler_params=pltpu.CompilerParams(
            dimension_semantics=("parallel", "arbitrary"),
            vmem_limit_bytes=_vmem_limit(vmem)),
        name="ffn_half_step",
    )(x, g, wg, wu, wd, wg_rem, wu_rem, wd_rem, fg)
    return outs if emit_bf16_weights else outs[0]


def _ffn_both(xp, xs, g, wg, wu, wd, fg, *, layer, apply_final_norm):
    d_ff = wg.shape[-1]
    n_main = d_ff // FFN_HIDDEN_TILE * FFN_HIDDEN_TILE
    w_rem = (wg[layer, :, n_main:].astype(BF16), wu[layer, :, n_main:].astype(BF16),
             wd[layer, n_main:, :].astype(BF16))
    xs, wg_b, wu_b, wd_b = _ffn(xs, g, wg, wu, wd, w_rem, fg, layer=layer, n_main=n_main,
                                apply_final_norm=apply_final_norm, emit_bf16_weights=True)
    xp = _ffn(xp, g, wg_b, wu_b, wd_b, w_rem, fg, layer=0, n_main=n_main,
              apply_final_norm=apply_final_norm)
    return xp, xs


def _inproj_body(x_ref, g_ref, w_ref, q_ref, k_ref, v_ref, u_ref, kb_ref, vb_ref, *, q_scale):
    h = _rmsnorm(x_ref[...], g_ref[...], NORM_EPS).astype(BF16)
    a = q_ref.shape[1]
    q_ref[...] = (jnp.dot(h, w_ref[:, 0:a], preferred_element_type=F32) * q_scale).astype(BF16)
    k = jnp.dot(h, w_ref[:, a:2 * a], preferred_element_type=F32)
    k_ref[...] = k
    kb_ref[...] = k.astype(BF16)
    v = jnp.dot(h, w_ref[:, 2 * a:3 * a], preferred_element_type=F32)
    v_ref[...] = v
    vb_ref[...] = v.astype(BF16)
    u_ref[...] = jnp.dot(h, w_ref[:, 3 * a:], preferred_element_type=F32)


def _inproj_outputs(m, a, pw):
    return [jax.ShapeDtypeStruct((m, a), BF16), jax.ShapeDtypeStruct((m, a), F32),
            jax.ShapeDtypeStruct((m, a), F32), jax.ShapeDtypeStruct((m, pw), F32),
            jax.ShapeDtypeStruct((m, a), BF16), jax.ShapeDtypeStruct((m, a), BF16)]


def _inproj_converting_body(x_ref, g_ref, w_ref, q_ref, k_ref, v_ref, u_ref, kb_ref, vb_ref,
                            wb_ref, h_ref, *, q_scale):
    j = pl.program_id(0)

    @pl.when(j == 0)
    def _():
        h_ref[...] = _rmsnorm(x_ref[...], g_ref[...], NORM_EPS).astype(BF16)

    w = w_ref[...].astype(BF16)
    wb_ref[...] = w
    y = jnp.dot(h_ref[...], w, preferred_element_type=F32)

    @pl.when(j == 0)
    def _():
        q_ref[...] = (y * q_scale).astype(BF16)

    @pl.when(j == 1)
    def _():
        k_ref[...] = y
        kb_ref[...] = y.astype(BF16)

    @pl.when(j == 2)
    def _():
        v_ref[...] = y
        vb_ref[...] = y.astype(BF16)

    @pl.when(j == 3)
    def _():
        u_ref[...] = y


def _inproj_converting(x, g, w_in, attn_width, *, layer):
    m, d = x.shape
    n = w_in.shape[-1]
    a = attn_width
    head_dim = a // (2 * N_HEADS)
    assert n == 4 * a and m <= PROJ_TOKEN_TILE
    whole = lambda j: (0, 0)
    vmem = (m * d * 4 + 2 * d * a * (4 + 2) + 2 * m * a * (2 + 4 + 4 + 4 + 2 + 2) + m * d * 2
            + m * a * 4)
    *outs, w_bf16 = pl.pallas_call(
        functools.partial(_inproj_converting_body, q_scale=head_dim ** -0.5),
        grid=(4,),
        in_specs=[
            pl.BlockSpec((m, d), whole, pipeline_mode=pl.Buffered(1)),
            pl.BlockSpec((1, d), whole),
            pl.BlockSpec((None, d, a), lambda j: (layer, 0, j)),
        ],
        out_specs=[pl.BlockSpec((m, a), whole)] * 6
                  + [pl.BlockSpec((None, d, a), lambda j: (0, 0, j))],
        out_shape=_inproj_outputs(m, a, a) + [jax.ShapeDtypeStruct((1, d, n), BF16)],
        scratch_shapes=[pltpu.VMEM((m, d), BF16)],
        compiler_params=pltpu.CompilerParams(
            dimension_semantics=("arbitrary",), vmem_limit_bytes=_vmem_limit(vmem)),
        name="input_projection_converting",
    )(x, g, w_in)
    return outs, w_bf16


def _inproj(x, g, w_in, attn_width, *, layer):
    m, d = x.shape
    n = w_in.shape[-1]
    a = attn_width
    pw = n - 3 * a
    head_dim = a // (2 * N_HEADS)
    tm = min(PROJ_TOKEN_TILE, m)
    assert m % tm == 0
    vmem = 2 * tm * d * 4 + d * n * 2 + 2 * tm * (a * 2 + 2 * a * 6 + pw * 4) + tm * n * 4
    row = lambda i: (i, 0)
    return pl.pallas_call(
        functools.partial(_inproj_body, q_scale=head_dim ** -0.5),
        grid=(m // tm,),
        in_specs=[
            pl.BlockSpec((tm, d), row),
            pl.BlockSpec((1, d), lambda i: (0, 0)),
            pl.BlockSpec((None, d, n), lambda i: (layer, 0, 0), pipeline_mode=pl.Buffered(1)),
        ],
        out_specs=[
            pl.BlockSpec((tm, a), row), pl.BlockSpec((tm, a), row), pl.BlockSpec((tm, a), row),
            pl.BlockSpec((tm, pw), row), pl.BlockSpec((tm, a), row), pl.BlockSpec((tm, a), row),
        ],
        out_shape=_inproj_outputs(m, a, pw),
        compiler_params=pltpu.CompilerParams(
            dimension_semantics=("parallel",), vmem_limit_bytes=_vmem_limit(vmem)),
        name="input_projection",
    )(x, g, w_in)


def _subln(o, gain, lam_init):
    return _rmsnorm(o, gain, SUBLN_EPS) * (1.0 - lam_init)


def _prompt_attn_body(lq1_ref, lk1_ref, lq2_ref, lk2_ref, sg_ref, q_ref, k_ref, v_ref, o_ref, *,
                      lam_init):
    seq, vd = q_ref.shape
    hd = vd // 2
    tq = ATTN_Q_TILE
    nt = (((1,), (1,)), ((), ()))
    lam = _diff_lambda(lq1_ref, lk1_ref, lq2_ref, lk2_ref, lam_init)
    lane = lax.broadcasted_iota(jnp.int32, (tq, vd), 1)
    row = lax.broadcasted_iota(jnp.int32, (2 * tq, tq), 0)
    row = jnp.where(row >= tq, row - tq, row)
    col = lax.broadcasted_iota(jnp.int32, (2 * tq, tq), 1)
    causal = col <= row

    for i in range(seq // tq):
        lo, hi = i * tq, (i + 1) * tq
        q = q_ref[lo:hi, :]
        zero = jnp.zeros_like(q)
        qz = jnp.concatenate([jnp.where(lane < hd, q, zero), jnp.where(lane >= hd, q, zero)],
                             axis=0)
        s_diag = lax.dot_general(qz, k_ref[lo:hi, :], nt, preferred_element_type=F32)
        s_diag = jnp.where(causal, s_diag, MASK_VALUE)
        m = jnp.max(s_diag, axis=-1, keepdims=True)
        if i > 0:
            s_past = lax.dot_general(qz, k_ref[0:lo, :], nt, preferred_element_type=F32)
            m = jnp.maximum(m, jnp.max(s_past, axis=-1, keepdims=True))
        p_diag = jnp.exp(s_diag - m)
        l = jnp.sum(p_diag, axis=-1, keepdims=True)
        acc = jnp.dot(p_diag.astype(BF16), v_ref[lo:hi, :], preferred_element_type=F32)
        if i > 0:
            p_past = jnp.exp(s_past - m)
            l = l + jnp.sum(p_past, axis=-1, keepdims=True)
            acc = acc + jnp.dot(p_past.astype(BF16), v_ref[0:lo, :], preferred_element_type=F32)
        on = acc * (1.0 / l)
        o = on[:tq] - lam * on[tq:]
        o_ref[lo:hi, :] = _subln(o, sg_ref[...], lam_init).astype(o_ref.dtype)


def _prompt_attention(lams, sg, qb, kb, vb, *, batch, seq, lam_init):
    m, a = qb.shape
    vd = a // N_HEADS
    tq = ATTN_Q_TILE
    assert seq % tq == 0
    lam_spec = pl.BlockSpec((1, vd // 2), lambda b, h: (0, 0))
    head_spec = pl.BlockSpec((seq, vd), lambda b, h: (b, h))
    vmem = 2 * 4 * seq * vd * 2 + 4 * 2 * tq * seq * 4
    return pl.pallas_call(
        functools.partial(_prompt_attn_body, lam_init=lam_init),
        grid=(batch, N_HEADS),
        in_specs=[lam_spec] * 4 + [pl.BlockSpec((1, vd), lambda b, h: (0, 0)),
                                   head_spec, head_spec, head_spec],
        out_specs=head_spec,
        out_shape=jax.ShapeDtypeStruct((m, a), BF16),
        compiler_params=pltpu.CompilerParams(
            dimension_semantics=("parallel", "parallel"),
            vmem_limit_bytes=_vmem_limit(vmem)),
        name="prompt_diff_attention",
    )(*lams, sg, qb, kb, vb)


def _sample_attn_body(pt_ref, lq1_ref, lk1_ref, lq2_ref, lk2_ref, sg_ref, q_ref, kn_ref, vn_ref,
                      *rest, n_pages, lam_init):
    del pt_ref
    k_pages = rest[:n_pages]
    v_pages = rest[n_pages:2 * n_pages]
    o_ref = rest[2 * n_pages]
    n_groups = SAMPLE_HEAD_GROUPS
    kbf_refs = rest[2 * n_pages + 1:2 * n_pages + 1 + n_groups]
    vbf_refs = rest[2 * n_pages + 1 + n_groups:]
    hg = N_HEADS // n_groups
    page_rows = k_pages[0].shape[0] // n_groups
    past_rows = n_pages * page_rows
    n_rows_k = kbf_refs[0].shape[0]
    n_new, a = q_ref.shape
    vd = a // N_HEADS
    hd = vd // 2

    lam = _diff_lambda(lq1_ref, lk1_ref, lq2_ref, lk2_ref, lam_init)
    qf = q_ref[...].astype(F32)
    kn, vn = kn_ref[...], vn_ref[...]
    n_q = 2 * hg * n_new
    r = lax.broadcasted_iota(jnp.int32, (n_q, vd), 0)
    c = lax.broadcasted_iota(jnp.int32, (n_q, vd), 1)
    q_half_mask = (c // hd) == (r // (hg * n_new))
    col = lax.broadcasted_iota(jnp.int32, (n_q, n_rows_k), 1)
    row = lax.broadcasted_iota(jnp.int32, (n_q, n_rows_k), 0)
    q_head, q_tok = (row // n_new) % hg, row % n_new
    new_col = col - past_rows
    visible = (((col < past_rows) & ((col % hg) == q_head))
               | ((new_col >= 0) & (new_col < hg * n_new) & ((new_col // n_new) == q_head)
                  & ((new_col % n_new) <= q_tok)))
    pad = jnp.zeros((n_rows_k - past_rows - hg * n_new, vd), F32)

    for g in range(n_groups):
        head_cols = [slice(h * vd, (h + 1) * vd) for h in range(g, N_HEADS, n_groups)]
        kbf_ref, vbf_ref = kbf_refs[g], vbf_refs[g]
        group_rows = pl.ds(g, page_rows, stride=n_groups)
        for j in range(n_pages):
            rows = slice(j * page_rows, (j + 1) * page_rows)
            kbf_ref[rows, :] = k_pages[j][group_rows, :].astype(BF16)
            vbf_ref[rows, :] = v_pages[j][group_rows, :].astype(BF16)
        kbf_ref[past_rows:, :] = jnp.concatenate([kn[:, hc] for hc in head_cols] + [pad],
                                                 axis=0).astype(BF16)
        vbf_ref[past_rows:, :] = jnp.concatenate([vn[:, hc] for hc in head_cols] + [pad],
                                                 axis=0).astype(BF16)

        q2 = jnp.concatenate([qf[:, hc] for hc in head_cols] * 2, axis=0)
        q2 = jnp.where(q_half_mask, q2, 0.0).astype(BF16)
        starts = list(range(0, past_rows, SAMPLE_CHUNK_PAGES * page_rows))
        chunks = [slice(lo, hi) for lo, hi in zip(starts, starts[1:] + [n_rows_k])]
        scores = [jnp.where(visible[:, ck],
                            lax.dot_general(q2, kbf_ref[ck, :], (((1,), (1,)), ((), ())),
                                            preferred_element_type=F32), MASK_VALUE)
                  for ck in chunks]
        m = functools.reduce(jnp.maximum, [jnp.max(s, axis=-1, keepdims=True) for s in scores])
        probs = [jnp.exp(s - m) for s in scores]
        inv_l = 1.0 / sum(jnp.sum(p, axis=-1, keepdims=True) for p in probs)
        w1, w2 = inv_l[:n_q // 2], lam * inv_l[n_q // 2:]
        o = sum(jnp.dot((p[:n_q // 2] * w1 - p[n_q // 2:] * w2).astype(BF16), vbf_ref[ck, :],
                        preferred_element_type=F32)
                for p, ck in zip(probs, chunks))
        o = _subln(o, sg_ref[...], lam_init)
        for i, hc in enumerate(head_cols):
            o_ref[:, hc] = o[i * n_new:(i + 1) * n_new]


def _sample_attention(page_table, lams, sg, q, k_new, v_new, k_pool, v_pool, *, page, first_page,
                      lam_init):
    nb, n_new, a = q.shape
    n_pages = page_table.shape[1]
    vd = a // N_HEADS
    page_rows = page * N_HEADS
    n_keys = n_pages * page + SAMPLE_KEY_PAD
    const = lambda b, pt: (0, 0)
    tok_spec = pl.BlockSpec((None, n_new, a), lambda b, pt: (b, 0, 0))
    page_specs = [pl.BlockSpec((page_rows, vd), lambda b, pt, j=j: (first_page + pt[b, j], 0))
                  for j in range(n_pages)]
    vmem = 2 * 2 * n_pages * page * a * 4 + 2 * n_keys * a * 2 + 8 * 16 * n_keys * 4
    return pl.pallas_call(
        functools.partial(_sample_attn_body, n_pages=n_pages, lam_init=lam_init),
        grid_spec=pltpu.PrefetchScalarGridSpec(
            num_scalar_prefetch=1,
            grid=(nb,),
            in_specs=[pl.BlockSpec((1, vd // 2), const)] * 4 + [pl.BlockSpec((1, vd), const)]
                     + [tok_spec] * 3 + page_specs + page_specs,
            out_specs=tok_spec,
            scratch_shapes=[pltpu.VMEM((n_keys * N_HEADS // SAMPLE_HEAD_GROUPS, vd), BF16)]
                           * (2 * SAMPLE_HEAD_GROUPS),
        ),
        out_shape=jax.ShapeDtypeStruct((nb, n_new, a), F32),
        compiler_params=pltpu.CompilerParams(
            dimension_semantics=("arbitrary",), vmem_limit_bytes=_vmem_limit(vmem)),
        name="sample_diff_attention",
    )(page_table, *lams, sg, q, k_new, v_new, *([k_pool] * n_pages), *([v_pool] * n_pages))


def _pool_project(d, wp_ref, scale_ref, g, cols):
    return jnp.dot(d.astype(BF16), wp_ref[g], preferred_element_type=F32) * scale_ref[:, cols]


def _prompt_pool_tile(i, u_ref, halo_ref, wp_ref, scale_ref, z_ref, ext_ref):
    tm, c = u_ref.shape
    cg = c // len(POOL_WINDOWS)
    ext_ref[0:POOL_HALO, :] = jnp.where(i == 0, 0.0, halo_ref[...])
    ext_ref[POOL_HALO:, :] = u_ref[...]
    pos = i * tm + lax.broadcasted_iota(jnp.int32, (tm, 1), 0)
    for g, w in enumerate(POOL_WINDOWS):
        cols = slice(g * cg, (g + 1) * cg)
        cur = ext_ref[POOL_HALO:, cols]
        win = cur
        for s in range(1, w):
            win = win + ext_ref[POOL_HALO - s:POOL_HALO - s + tm, cols]
        cnt = jnp.minimum(w, pos + 1).astype(F32)
        d = win * (1.0 / cnt) - cur
        z_ref[:, cols] = _pool_project(d, wp_ref, scale_ref, g, cols).astype(z_ref.dtype)


def _sample_pool_body(state_ref, u_ref, wp_ref, scale_ref, z_ref, *, pos0):
    n_state = state_ref.shape[0]
    n_new, _, c = u_ref.shape
    cg = c // len(POOL_WINDOWS)

    def ext(r, cols):
        return state_ref[r, :, cols] if r < n_state else u_ref[r - n_state, :, cols]

    for g, w in enumerate(POOL_WINDOWS):
        cols = slice(g * cg, (g + 1) * cg)
        for t in range(n_new):
            cur = ext(n_state + t, cols)
            win = cur
            for s in range(1, w):
                win = win + ext(n_state + t - s, cols)
            d = win * (1.0 / min(w, pos0 + t + 1)) - cur
            z_ref[t, :, cols] = _pool_project(d, wp_ref, scale_ref, g, cols).astype(z_ref.dtype)


def _sample_pool(state_t, u_t, wp, scale, *, pos0):
    n_new, nb, c = u_t.shape
    vmem = 2 * (state_t.size + u_t.size) * 4 + wp.size * 4 + 4 * nb * c * 4
    return pl.pallas_call(
        functools.partial(_sample_pool_body, pos0=pos0),
        out_shape=jax.ShapeDtypeStruct((n_new, nb, c), BF16),
        compiler_params=pltpu.CompilerParams(vmem_limit_bytes=_vmem_limit(vmem)),
        name="sample_pool_mixer",
    )(state_t, u_t, wp, scale)


def _outproj_body(x_ref, o_ref, z_ref, w_ref, y_ref):
    a = o_ref.shape[1]
    y = x_ref[...] + jnp.dot(o_ref[...].astype(BF16), w_ref[0:a, :], preferred_element_type=F32)
    y_ref[...] = y + jnp.dot(z_ref[...].astype(BF16), w_ref[a:, :], preferred_element_type=F32)


def _outproj(x, o, z, w_out):
    m, d = x.shape
    a, c = o.shape[1], z.shape[1]
    tm = min(PROJ_TOKEN_TILE, m)
    assert m % tm == 0
    row = lambda i: (i, 0)
    vmem = 2 * 2 * tm * d * 4 + 2 * tm * (a + c) * 4 + w_out.size * 2 + tm * d * 4
    return pl.pallas_call(
        _outproj_body,
        grid=(m // tm,),
        in_specs=[
            pl.BlockSpec((tm, d), row), pl.BlockSpec((tm, a), row), pl.BlockSpec((tm, c), row),
            pl.BlockSpec(w_out.shape, lambda i: (0, 0), pipeline_mode=pl.Buffered(1)),
        ],
        out_specs=pl.BlockSpec((tm, d), row),
        out_shape=jax.ShapeDtypeStruct((m, d), F32),
        compiler_params=pltpu.CompilerParams(
            dimension_semantics=("parallel",), vmem_limit_bytes=_vmem_limit(vmem)),
        name="output_projection",
    )(x, o, z, w_out)


def _prompt_outproj_body(x_ref, o_ref, u_ref, halo_ref, wp_ref, scale_ref, w_ref, y_ref,
                         ext_ref, z_ref):
    a = o_ref.shape[1]
    y = x_ref[...] + jnp.dot(o_ref[...], w_ref[0:a, :], preferred_element_type=F32)
    _prompt_pool_tile(pl.program_id(1), u_ref, halo_ref, wp_ref, scale_ref, z_ref, ext_ref)
    y_ref[...] = y + jnp.dot(z_ref[...], w_ref[a:, :], preferred_element_type=F32)


def _prompt_outproj(x, o, u, wp, scale, w_out, *, batch, seq):
    m, d = x.shape
    a, c = o.shape[1], u.shape[1]
    tm = POOL_TOKEN_TILE
    nt = seq // tm
    halo_per_tile = tm // POOL_HALO
    assert seq % tm == 0 and o.dtype == BF16
    row = lambda b, i: (b * nt + i, 0)
    const = lambda b, i: (0, 0)
    vmem = (2 * 2 * tm * d * 4 + 2 * tm * (a * 2 + c * 4) + w_out.size * 2 + wp.size * 2 * 2
            + (tm + POOL_HALO) * c * 4 + tm * c * 2 + tm * d * 4 + 4 * tm * c * 4)
    return pl.pallas_call(
        _prompt_outproj_body,
        grid=(batch, nt),
        in_specs=[
            pl.BlockSpec((tm, d), row), pl.BlockSpec((tm, a), row), pl.BlockSpec((tm, c), row),
            pl.BlockSpec((POOL_HALO, c),
                         lambda b, i: (jnp.maximum((b * nt + i) * halo_per_tile - 1, 0), 0)),
            pl.BlockSpec(wp.shape, lambda b, i: (0, 0, 0)),
            pl.BlockSpec((1, c), const),
            pl.BlockSpec(w_out.shape, const, pipeline_mode=pl.Buffered(1)),
        ],
        out_specs=pl.BlockSpec((tm, d), row),
        out_shape=jax.ShapeDtypeStruct((m, d), F32),
        scratch_shapes=[pltpu.VMEM((tm + POOL_HALO, c), F32), pltpu.VMEM((tm, c), BF16)],
        compiler_params=pltpu.CompilerParams(
            dimension_semantics=("parallel", "arbitrary"), vmem_limit_bytes=_vmem_limit(vmem)),
        name="prompt_pool_output_projection",
    )(x, o, u, u, wp, scale, w_out)


def kernel(x_prompt, x_sample, cache_k, cache_v, state_pool, page_table, ffn1_norm, ffn1_w_gate, ffn1_w_up, ffn1_w_down, mix_norm, w_in, lambda_q1, lambda_k1, lambda_q2, lambda_k2, subln_gain, w_pool, pool_scale, w_out, ffn2_norm, ffn2_w_gate, ffn2_w_up, ffn2_w_down, final_norm):
    batch, seq, d = x_prompt.shape
    nb, n_new, _ = x_sample.shape
    depth, n_phys, page, n_heads, vd = cache_k.shape
    assert n_heads == N_HEADS
    a = n_heads * vd
    c = state_pool.shape[-1]
    past_len = page_table.shape[1] * page
    k_pool = cache_k.reshape(depth * n_phys * page * n_heads, vd)
    v_pool = cache_v.reshape(depth * n_phys * page * n_heads, vd)

    xp = x_prompt.reshape(batch * seq, d)
    xs = x_sample.reshape(nb * n_new, d)
    fg = final_norm.reshape(1, d)
    k_p, v_p, buf_p, k_s, v_s, buf_s = [], [], [], [], [], []
    for l in range(depth):
        lam_init = _lambda_init(l)
        last = l == depth - 1
        mix_g = mix_norm[l].reshape(1, d)
        w_out_b = w_out[l].astype(BF16)
        wp_b = w_pool[l].astype(BF16)
        scale = pool_scale[l].reshape(1, c)
        lams = tuple(p[l].reshape(1, -1) for p in (lambda_q1, lambda_k1, lambda_q2, lambda_k2))
        sg = subln_gain[l].reshape(1, vd)

        xp, xs = _ffn_both(xp, xs, ffn1_norm[l].reshape(1, d), ffn1_w_gate, ffn1_w_up, ffn1_w_down,
                           fg, layer=l, apply_final_norm=False)

        (qs, ks, vs, us, _, _), w_in_b = _inproj_converting(xs, mix_g, w_in, a, layer=l)

        qb, kp, vp, up, kb, vb = _inproj(xp, mix_g, w_in_b, a, layer=0)
        op = _prompt_attention(lams, sg, qb, kb, vb, batch=batch, seq=seq, lam_init=lam_init)
        xp = _prompt_outproj(xp, op, up, wp_b, scale, w_out_b, batch=batch, seq=seq)

        os_ = _sample_attention(
            page_table, lams, sg, qs.reshape(nb, n_new, a), ks.reshape(nb, n_new, a),
            vs.reshape(nb, n_new, a), k_pool, v_pool, page=page,
            first_page=l * n_phys, lam_init=lam_init)
        us3 = us.reshape(nb, n_new, c)
        zs_t = _sample_pool(jnp.transpose(state_pool[l], (1, 0, 2)), jnp.transpose(us3, (1, 0, 2)),
                            wp_b, scale, pos0=past_len)
        zs = jnp.transpose(zs_t, (1, 0, 2)).reshape(nb * n_new, c)
        xs = _outproj(xs, os_.reshape(nb * n_new, a), zs, w_out_b)

        xp, xs = _ffn_both(xp, xs, ffn2_norm[l].reshape(1, d), ffn2_w_gate, ffn2_w_up, ffn2_w_down,
                           fg, layer=l, apply_final_norm=last)

        k_p.append(kp.reshape(batch, seq, n_heads, vd))
        v_p.append(vp.reshape(batch, seq, n_heads, vd))
        buf_p.append(up.reshape(batch, seq, c)[:, seq - POOL_BUF:])
        k_s.append(ks.reshape(nb, n_new, n_heads, vd))
        v_s.append(vs.reshape(nb, n_new, n_heads, vd))
        buf_s.append(jnp.concatenate([state_pool[l], us3], axis=1)[:, -POOL_BUF:])

    return (xp.reshape(batch, seq, d), xs.reshape(nb, n_new, d),
            jnp.stack(k_p), jnp.stack(v_p), jnp.stack(buf_p),
            jnp.stack(k_s), jnp.stack(v_s), jnp.stack(buf_s))
```
